```python
import jax, jax.numpy as jnp
from jax import lax
import numpy as np

D_MODEL = 1024
BATCH = 8
SEQ = 8192
DEPTH = 4

EPS = 1e-6
FFN_DIM = 1408
Q_BLOCK = 128

SB_HEADS = 4
SB_HEAD_DIM = 64
SB_WIDTH = SB_HEADS * SB_HEAD_DIM

ML_HEADS = 4
ML_HEAD_DIM = 128
ML_WIDTH = ML_HEADS * ML_HEAD_DIM
ML_CHUNK = 128
ML_CONV = 4
ML_FORGET_BIAS_LO = 3.0
ML_FORGET_BIAS_HI = 6.0

MLA_HEADS = 4
MLA_NOPE = 64
MLA_ROPE = 32
MLA_QK = MLA_NOPE + MLA_ROPE
MLA_V = 64
MLA_WIDTH = MLA_HEADS * MLA_V
MLA_Q_RANK = 256
MLA_KV_RANK = 128
ROPE_THETA = 10000.0

N_BRANCH = 3
SEGMENTS = (
    ('sb_q', SB_WIDTH), ('sb_k', SB_WIDTH), ('sb_v', SB_WIDTH),
    ('ml_q', ML_WIDTH), ('ml_k', ML_WIDTH), ('ml_v', ML_WIDTH), ('ml_o', ML_WIDTH),
    ('ml_i', ML_HEADS), ('ml_f', ML_HEADS),
    ('mla_cq', MLA_Q_RANK), ('mla_ckv', MLA_KV_RANK), ('mla_kr', MLA_ROPE),
    ('gates', N_BRANCH * D_MODEL),
)
N_IN = sum(w for _, w in SEGMENTS)

kernel_name = 'hybrid_sb_mlstm_mla_macaron'


def _seg_range(name):
    off = 0
    for n, w in SEGMENTS:
        if n == name:
            return off, off + w
        off += w
    raise KeyError(name)


def _split_cols(z):
    out = {}
    off = 0
    for n, w in SEGMENTS:
        out[n] = z[..., off:off + w]
        off += w
    return out


def _rms_norm(x, gain):
    xf = x.astype(jnp.float32)
    y = xf * lax.rsqrt(jnp.mean(xf * xf, axis=-1, keepdims=True) + EPS)
    return (y * gain.astype(jnp.float32)).astype(x.dtype)


def _swiglu(u, wi, wo):
    a, g = jnp.split(u @ wi, 2, axis=-1)
    return (jax.nn.silu(a) * g) @ wo


def _to_heads(z, n_heads):
    b, s, w = z.shape
    return z.reshape(b, s, n_heads, w // n_heads).transpose(0, 2, 1, 3)


def _from_heads(z):
    b, h, s, d = z.shape
    return z.transpose(0, 2, 1, 3).reshape(b, s, h * d)


def _strict_lower(n):
    return jnp.asarray(np.tril(np.ones((n, n), np.float32), -1))


def _causal_mask(i, n, strict):
    qpos = i * Q_BLOCK + np.arange(Q_BLOCK)
    kpos = np.arange(n)
    m = kpos[None, :] < qpos[:, None] if strict else kpos[None, :] <= qpos[:, None]
    return jnp.asarray(m)


def _stick_breaking_attention(q, k, v):
    b, h, s, _ = q.shape
    scale = SB_HEAD_DIM ** -0.5
    tri_in = _strict_lower(Q_BLOCK)
    outs = []
    for i in range(s // Q_BLOCK):
        nk = i + 1
        n = nk * Q_BLOCK
        qb = q[:, :, i * Q_BLOCK:n]
        z = jnp.einsum('bhqd,bhkd->bhqk', qb, k[:, :, :n], preferred_element_type=jnp.float32) * scale
        mask = _causal_mask(i, n, strict=True)
        log_keep = jnp.where(mask, jax.nn.log_sigmoid(-z), 0.0)
        lk = log_keep.reshape(b, h, Q_BLOCK, nk, Q_BLOCK)
        within = jnp.einsum('bhqnj,js->bhqns', lk, tri_in)
        after = jnp.einsum('bhqm,mn->bhqn', jnp.sum(lk, axis=-1), _strict_lower(nk))
        later = (within + after[..., None]).reshape(b, h, Q_BLOCK, n)
        w = jnp.where(mask, jnp.exp(jax.nn.log_sigmoid(z) + later), 0.0)
        outs.append(jnp.einsum('bhqk,bhkd->bhqd', w.astype(v.dtype), v[:, :, :n]))
    return jnp.concatenate(outs, axis=2)


def _causal_softmax_attention(q, k, v, scale):
    s = q.shape[2]
    outs = []
    for i in range(s // Q_BLOCK):
        n = (i + 1) * Q_BLOCK
        qb = q[:, :, i * Q_BLOCK:n]
        z = jnp.einsum('bhqd,bhkd->bhqk', qb, k[:, :, :n], preferred_element_type=jnp.float32) * scale
        z = jnp.where(_causal_mask(i, n, strict=False), z, -jnp.inf)
        p = jnp.exp(z - jnp.max(z, axis=-1, keepdims=True))
        denom = jnp.sum(p, axis=-1, keepdims=True)
        o = jnp.einsum('bhqk,bhkd->bhqd', p.astype(v.dtype), v[:, :, :n], preferred_element_type=jnp.float32)
        outs.append((o / denom).astype(v.dtype))
    return jnp.concatenate(outs, axis=2)


def _rope(x, positions):
    half = MLA_ROPE // 2
    inv_freq = jnp.power(ROPE_THETA, -jnp.arange(half, dtype=jnp.float32) / half)
    ang = positions.astype(jnp.float32)[:, None, :, None] * inv_freq
    cos, sin = jnp.cos(ang), jnp.sin(ang)
    xf = x.astype(jnp.float32)
    x1, x2 = xf[..., :half], xf[..., half:]
    return jnp.concatenate([x1 * cos - x2 * sin, x2 * cos + x1 * sin], axis=-1).astype(x.dtype)


def _mla(c_q, c_kv, k_rope, positions, q_norm, kv_norm, wq_up, wkv_up, q_gain, k_gain):
    b, s, _ = c_q.shape
    q = (_rms_norm(c_q, q_norm) @ wq_up).reshape(b, s, MLA_HEADS, MLA_QK)
    kv = (_rms_norm(c_kv, kv_norm) @ wkv_up).reshape(b, s, MLA_HEADS, MLA_NOPE + MLA_V)
    k_nope, v = kv[..., :MLA_NOPE], kv[..., MLA_NOPE:]
    k_pe = jnp.broadcast_to(k_rope[:, :, None, :], (b, s, MLA_HEADS, MLA_ROPE))
    k = jnp.concatenate([k_nope, k_pe], axis=-1)
    q = _rms_norm(q, q_gain).transpose(0, 2, 1, 3)
    k = _rms_norm(k, k_gain).transpose(0, 2, 1, 3)
    v = v.transpose(0, 2, 1, 3)
    q = jnp.concatenate([q[..., :MLA_NOPE], _rope(q[..., MLA_NOPE:], positions)], axis=-1)
    k = jnp.concatenate([k[..., :MLA_NOPE], _rope(k[..., MLA_NOPE:], positions)], axis=-1)
    return _from_heads(_causal_softmax_attention(q, k, v, MLA_QK ** -0.5))


def _causal_dwconv(x, w, bias):
    y = lax.conv_general_dilated(
        x, w[:, None, :], window_strides=(1,), padding=[(ML_CONV - 1, 0)],
        dimension_numbers=('NWC', 'WIO', 'NWC'), feature_group_count=x.shape[-1])
    return y + bias


def _mlstm_chunkwise(q, k, v, ig, lf):
    b, h, s, dk = q.shape
    dv = v.shape[-1]
    nc = s // ML_CHUNK

    def chunks(z):
        return jnp.moveaxis(z.reshape((b, h, nc, ML_CHUNK) + z.shape[3:]), 2, 0)

    causal = jnp.asarray(np.tril(np.ones((ML_CHUNK, ML_CHUNK), dtype=bool)))
    lower_incl = jnp.asarray(np.tril(np.ones((ML_CHUNK, ML_CHUNK), np.float32)))

    def step(carry, xs):
        c_prev, n_prev, m_prev = carry
        qc, kc, vc, igc, lfc = xs
        cum_f = jnp.einsum('bhs,ts->bht', lfc, lower_incl)
        log_intra = jnp.where(causal, cum_f[..., :, None] - cum_f[..., None, :] + igc[..., None, :], -jnp.inf)
        log_inter = cum_f + m_prev[..., None]
        m = jnp.maximum(log_inter, jnp.max(log_intra, axis=-1))
        w_intra = jnp.exp(log_intra - m[..., None])
        w_inter = jnp.exp(log_inter - m)
        scores = jnp.einsum('bhtd,bhsd->bhts', qc, kc) * w_intra
        num = scores @ vc + w_inter[..., None] * (qc @ c_prev)
        den = jnp.sum(scores, axis=-1) + w_inter * jnp.einsum('bhtd,bhd->bht', qc, n_prev)
        h_out = num / jnp.maximum(jnp.abs(den), jnp.exp(-m))[..., None]
        f_total = cum_f[..., -1]
        log_to_end = f_total[..., None] - cum_f + igc
        m_new = jnp.maximum(f_total + m_prev, jnp.max(log_to_end, axis=-1))
        decay = jnp.exp(f_total + m_prev - m_new)
        w_end = jnp.exp(log_to_end - m_new[..., None])
        c_new = decay[..., None, None] * c_prev + jnp.einsum('bhsd,bhsv->bhdv', kc * w_end[..., None], vc)
        n_new = decay[..., None] * n_prev + jnp.einsum('bhs,bhsd->bhd', w_end, kc)
        return (c_new, n_new, m_new), h_out

    f32 = jnp.float32
    init = (jnp.zeros((b, h, dk, dv), f32), jnp.zeros((b, h, dk), f32), jnp.zeros((b, h), f32))
    _, hs = lax.scan(step, init, (chunks(q), chunks(k), chunks(v), chunks(ig), chunks(lf)))
    return jnp.moveaxis(hs, 0, 2).reshape(b, h, s, dv)


def _mlstm(q, k, v, o_pre, i_pre, f_pre, conv_w, conv_b, out_gain):
    b, s, _ = q.shape
    f32 = jnp.float32
    qk = jax.nn.silu(_causal_dwconv(jnp.concatenate([q, k], axis=-1), conv_w, conv_b))
    q, k = jnp.split(qk, 2, axis=-1)
    qh = _to_heads(q, ML_HEADS).astype(f32)
    kh = _to_heads(k, ML_HEADS).astype(f32) * ML_HEAD_DIM ** -0.5
    vh = _to_heads(v, ML_HEADS).astype(f32)
    ig = i_pre.astype(f32).transpose(0, 2, 1)
    lf = jax.nn.log_sigmoid(f_pre.astype(f32)).transpose(0, 2, 1)
    hh = _mlstm_chunkwise(qh, kh, vh, ig, lf).transpose(0, 2, 1, 3)
    hh = _rms_norm(hh, out_gain.reshape(ML_HEADS, ML_HEAD_DIM)).reshape(b, s, ML_WIDTH)
    return (jax.nn.sigmoid(o_pre.astype(f32)) * hh).astype(v.dtype)


def setup_inputs(seed: int = 0) -> dict:
    key = jax.random.key(seed)
    ks = jax.random.split(key, 32)
    f32 = jnp.float32

    def normal(k, shape, fan_in):
        return jax.random.normal(k, shape, f32) * fan_in ** -0.5

    def gain(k, shape):
        return 1.0 + 0.02 * jax.random.normal(k, shape, f32)

    L, D = DEPTH, D_MODEL
    x = jax.random.normal(ks[0], (BATCH, SEQ, D), f32)
    offsets = jax.random.randint(ks[1], (BATCH, 1), 0, 4096, dtype=jnp.int32)
    positions = offsets + jnp.arange(SEQ, dtype=jnp.int32)[None, :]

    b_in = 0.01 * jax.random.normal(ks[2], (L, N_IN), f32)
    f0, f1 = _seg_range('ml_f')
    b_in = b_in.at[:, f0:f1].add(jnp.linspace(ML_FORGET_BIAS_LO, ML_FORGET_BIAS_HI, ML_HEADS, dtype=f32))

    return {
        'x': x,
        'positions': positions,
        'ffn1_norm': gain(ks[3], (L, D)),
        'ffn1_wi': normal(ks[4], (L, D, 2 * FFN_DIM), D),
        'ffn1_wo': normal(ks[5], (L, FFN_DIM, D), FFN_DIM),
        'mix_norm': gain(ks[6], (L, D)),
        'w_in': normal(ks[7], (L, D, N_IN), D),
        'b_in': b_in,
        'ml_conv_w': normal(ks[8], (L, ML_CONV, 2 * ML_WIDTH), ML_CONV),
        'ml_conv_b': 0.01 * jax.random.normal(ks[9], (L, 2 * ML_WIDTH), f32),
        'ml_out_norm': gain(ks[10], (L, ML_WIDTH)),
        'mla_q_norm': gain(ks[11], (L, MLA_Q_RANK)),
        'mla_kv_norm': gain(ks[12], (L, MLA_KV_RANK)),
        'mla_wq_up': normal(ks[13], (L, MLA_Q_RANK, MLA_HEADS * MLA_QK), MLA_Q_RANK),
        'mla_wkv_up': normal(ks[14], (L, MLA_KV_RANK, MLA_HEADS * (MLA_NOPE + MLA_V)), MLA_KV_RANK),
        'mla_q_gain': gain(ks[15], (L, MLA_QK)),
        'mla_k_gain': gain(ks[16], (L, MLA_QK)),
        'w_up_sb': normal(ks[17], (L, SB_WIDTH, D), SB_WIDTH),
        'w_up_ml': normal(ks[18], (L, ML_WIDTH, D), ML_WIDTH),
        'w_up_mla': normal(ks[19], (L, MLA_WIDTH, D), MLA_WIDTH),
        'w_out': normal(ks[20], (L, D, D), D),
        'ffn2_norm': gain(ks[21], (L, D)),
        'ffn2_wi': normal(ks[22], (L, D, 2 * FFN_DIM), D),
        'ffn2_wo': normal(ks[23], (L, FFN_DIM, D), FFN_DIM),
    }


def reference(x, positions, ffn1_norm, ffn1_wi, ffn1_wo, mix_norm, w_in, b_in, ml_conv_w, ml_conv_b,
              ml_out_norm, mla_q_norm, mla_kv_norm, mla_wq_up, mla_wkv_up, mla_q_gain, mla_k_gain,
              w_up_sb, w_up_ml, w_up_mla, w_out, ffn2_norm, ffn2_wi, ffn2_wo):
    b, s, d = x.shape
    for l in range(DEPTH):
        x = x + 0.5 * _swiglu(_rms_norm(x, ffn1_norm[l]), ffn1_wi[l], ffn1_wo[l])

        u = _rms_norm(x, mix_norm[l])
        c = _split_cols(u @ w_in[l] + b_in[l])

        y_sb = _from_heads(_stick_breaking_attention(
            _to_heads(c['sb_q'], SB_HEADS), _to_heads(c['sb_k'], SB_HEADS), _to_heads(c['sb_v'], SB_HEADS)))
        y_ml = _mlstm(c['ml_q'], c['ml_k'], c['ml_v'], c['ml_o'], c['ml_i'], c['ml_f'],
                      ml_conv_w[l], ml_conv_b[l], ml_out_norm[l])
        y_mla = _mla(c['mla_cq'], c['mla_ckv'], c['mla_kr'], positions, mla_q_norm[l], mla_kv_norm[l],
                     mla_wq_up[l], mla_wkv_up[l], mla_q_gain[l], mla_k_gain[l])

        g = jax.nn.sigmoid(c['gates'].astype(jnp.float32)).astype(x.dtype).reshape(b, s, N_BRANCH, d)
        merged = (g[:, :, 0] * (y_sb @ w_up_sb[l])
                  + g[:, :, 1] * (y_ml @ w_up_ml[l])
                  + g[:, :, 2] * (y_mla @ w_up_mla[l]))
        x = x + merged @ w_out[l]

        x = x + 0.5 * _swiglu(_rms_norm(x, ffn2_norm[l]), ffn2_wi[l], ffn2_wo[l])
    return x
```

```python
import functools

import numpy as np
import jax
import jax.numpy as jnp
from jax import lax
from jax.experimental import pallas as pl
from jax.experimental.pallas import tpu as pltpu

F32 = jnp.float32
BF16 = jnp.bfloat16

EPS = 1e-6
LANES = 128
V7X_VMEM_LIMIT = 56 * 1024 * 1024

D_MODEL = 1024
FFN_DIM = 1408
SB_HEADS, SB_HEAD_DIM = 4, 64
SB_WIDTH = SB_HEADS * SB_HEAD_DIM
ML_HEADS, ML_HEAD_DIM = 4, 128
ML_WIDTH = ML_HEADS * ML_HEAD_DIM
ML_CHUNK = 128
ML_CONV = 4
MLA_HEADS, MLA_NOPE, MLA_ROPE, MLA_V = 4, 64, 32, 64
MLA_QK = MLA_NOPE + MLA_ROPE
MLA_WIDTH = MLA_HEADS * MLA_V
MLA_Q_RANK, MLA_KV_RANK = 256, 128
ROPE_THETA = 10000.0
N_BRANCH = 3

SEGMENTS = (
    ('sb_q', SB_WIDTH), ('sb_k', SB_WIDTH), ('sb_v', SB_WIDTH),
    ('ml_q', ML_WIDTH), ('ml_k', ML_WIDTH), ('ml_v', ML_WIDTH), ('ml_o', ML_WIDTH),
    ('ml_i', ML_HEADS), ('ml_f', ML_HEADS),
    ('mla_cq', MLA_Q_RANK), ('mla_ckv', MLA_KV_RANK), ('mla_kr', MLA_ROPE),
    ('gates', N_BRANCH * D_MODEL),
)

W_GATES = N_BRANCH * D_MODEL
W_MLQK = 2 * ML_WIDTH
W_MLVO = 2 * ML_WIDTH
W_SB = 3 * SB_WIDTH
W_MLA = MLA_Q_RANK + MLA_KV_RANK + LANES
W_IF = LANES
OUT_WIDTHS = (W_GATES, W_MLQK, W_MLVO, W_SB, W_MLA, W_IF)
N_PACKED = sum(OUT_WIDTHS)

TM_DENSE = 512
TQ_ATTN = 256
SB_SUB = 128


def _cparams(sem):
    return pltpu.CompilerParams(dimension_semantics=sem, vmem_limit_bytes=V7X_VMEM_LIMIT)


def _const_spec(shape):
    nd = len(shape)
    return pl.BlockSpec(shape, lambda *_: (0,) * nd)


def _rms(x, gain):
    return x * lax.rsqrt(jnp.mean(x * x, axis=-1, keepdims=True) + EPS) * gain


def _dot(a, b):
    return jnp.dot(a, b, preferred_element_type=F32)


def _dot_nt(a, b):
    return lax.dot_general(a, b, (((1,), (1,)), ((), ())), preferred_element_type=F32)


def _dot_tn(a, b):
    return lax.dot_general(a, b, (((0,), (0,)), ((), ())), preferred_element_type=F32)


def _swiglu_residual(x, gain, wi_ref, wo_ref):
    u = _rms(x, gain).astype(BF16)
    h = _dot(u, wi_ref[...])
    a = h[:, :FFN_DIM]
    g = h[:, FFN_DIM:]
    act = (a * jax.nn.sigmoid(a) * g).astype(BF16)
    return x + 0.5 * _dot(act, wo_ref[...])


def _ffn_body(x_ref, g_ref, wi_ref, wo_ref, o_ref):
    o_ref[...] = _swiglu_residual(x_ref[...], g_ref[...], wi_ref, wo_ref)


def _ffn(x, gain, wi, wo):
    t, d = x.shape
    tm = min(TM_DENSE, t)
    return pl.pallas_call(
        _ffn_body,
        grid=(t // tm,),
        in_specs=[pl.BlockSpec((tm, d), lambda i: (i, 0)),
                  _const_spec(gain.shape), _const_spec(wi.shape), _const_spec(wo.shape)],
        out_specs=pl.BlockSpec((tm, d), lambda i: (i, 0)),
        out_shape=jax.ShapeDtypeStruct((t, d), F32),
        compiler_params=_cparams(("parallel",)),
        name="ffn",
    )(x, gain, wi, wo)


def _inproj_body(x_ref, g_ref, w_ref, b_ref, *out_refs):
    u = _rms(x_ref[...], g_ref[...]).astype(BF16)
    off = 0
    for o_ref, width in zip(out_refs, OUT_WIDTHS):
        step = min(width, 1024)
        for c in range(0, width, step):
            z = _dot(u, w_ref[:, off + c:off + c + step]) + b_ref[:, off + c:off + c + step]
            o_ref[:, c:c + step] = z.astype(o_ref.dtype)
        off += width


def _inproj(x, gain, w, b):
    t, d = x.shape
    tm = min(TM_DENSE, t)
    dtypes = (BF16, BF16, BF16, BF16, BF16, F32)
    return pl.pallas_call(
        _inproj_body,
        grid=(t // tm,),
        in_specs=[pl.BlockSpec((tm, d), lambda i: (i, 0)),
                  _const_spec(gain.shape), _const_spec(w.shape), _const_spec(b.shape)],
        out_specs=[pl.BlockSpec((tm, wd), lambda i: (i, 0)) for wd in OUT_WIDTHS],
        out_shape=[jax.ShapeDtypeStruct((t, wd), dt) for wd, dt in zip(OUT_WIDTHS, dtypes)],
        compiler_params=_cparams(("parallel",)),
        name="inproj",
    )(x, gain, w, b)


def _merge_body(x_ref, gt_ref, ysb_ref, yml_ref, ymla_ref, wsb_ref, wml_ref, wmla_ref, wout_ref,
                g2_ref, wi_ref, wo_ref, o_ref):
    d = D_MODEL
    merged = jax.nn.sigmoid(gt_ref[:, 0:d].astype(F32)) * _dot(ysb_ref[...], wsb_ref[...])
    merged += jax.nn.sigmoid(gt_ref[:, d:2 * d].astype(F32)) * _dot(yml_ref[...], wml_ref[...])
    merged += jax.nn.sigmoid(gt_ref[:, 2 * d:3 * d].astype(F32)) * _dot(ymla_ref[...], wmla_ref[...])
    x = x_ref[...] + _dot(merged.astype(BF16), wout_ref[...])
    o_ref[...] = _swiglu_residual(x, g2_ref[...], wi_ref, wo_ref)


def _merge_ffn(x, gates, ysb, yml, ymla, wsb, wml, wmla, wout, g2, wi, wo):
    t, d = x.shape
    tm = min(TM_DENSE, t)
    row = lambda a: pl.BlockSpec((tm, a.shape[1]), lambda i: (i, 0))
    consts = (wsb, wml, wmla, wout, g2, wi, wo)
    return pl.pallas_call(
        _merge_body,
        grid=(t // tm,),
        in_specs=[row(x), row(gates), row(ysb), row(yml), row(ymla)] + [_const_spec(c.shape) for c in consts],
        out_specs=pl.BlockSpec((tm, d), lambda i: (i, 0)),
        out_shape=jax.ShapeDtypeStruct((t, d), F32),
        compiler_params=_cparams(("parallel",)),
        name="merge_ffn",
    )(x, gates, ysb, yml, ymla, *consts)


def _sb_body(q_ref, k_ref, v_ref, o_ref, *, tq):
    qi = pl.program_id(2)
    nsub = tq // SB_SUB
    lane = lax.broadcasted_iota(jnp.int32, (1, LANES), 1)
    jj = lax.broadcasted_iota(jnp.int32, (SB_SUB, 2 * SB_SUB), 0)
    ss = lax.broadcasted_iota(jnp.int32, (SB_SUB, 2 * SB_SUB), 1)
    tri2 = jnp.where((jj > ss) | (ss >= SB_SUB), 1.0, 0.0).astype(BF16)
    qpos = lax.broadcasted_iota(jnp.int32, (tq, SB_SUB), 0)
    kpos = lax.broadcasted_iota(jnp.int32, (tq, SB_SUB), 1)

    q = q_ref[0]
    scale = SB_HEAD_DIM ** -0.5
    qs = [jnp.where((lane >= h * SB_HEAD_DIM) & (lane < (h + 1) * SB_HEAD_DIM), q, 0) * scale
          for h in range(2)]

    def sweep(start, carry, diag_sub):
        kc = k_ref[0, pl.ds(start, SB_SUB), :]
        vc = v_ref[0, pl.ds(start, SB_SUB), :]
        new = []
        for h in range(2):
            run, acc = carry[h]
            s = _dot_nt(qs[h], kc)
            ls = jnp.minimum(s, 0.0) - jnp.log(1.0 + jnp.exp(-jnp.abs(s)))
            lk = ls - s
            if diag_sub is not None:
                mask = kpos + diag_sub * SB_SUB < qpos
                lk = jnp.where(mask, lk, 0.0)
            wt = _dot(lk.astype(BF16), tri2)
            w = jnp.exp(ls + wt[:, :SB_SUB] + run)
            if diag_sub is not None:
                w = jnp.where(mask, w, 0.0)
            acc = acc + _dot(w.astype(BF16), vc)
            run = run + wt[:, SB_SUB:]
            new.append((run, acc))
        return tuple(new)

    zero = jnp.zeros((tq, LANES), F32)
    carry = ((zero, zero), (zero, zero))
    for c in reversed(range(nsub)):
        carry = sweep(pl.multiple_of(qi * tq + c * SB_SUB, SB_SUB), carry, c)

    def tile(i, carry):
        kb = qi - 1 - i
        for c in reversed(range(nsub)):
            carry = sweep(pl.multiple_of(kb * tq + c * SB_SUB, SB_SUB), carry, None)
        return carry

    carry = lax.fori_loop(0, qi, tile, carry)
    o_ref[0] = jnp.where(lane < SB_HEAD_DIM, carry[0][1], carry[1][1]).astype(o_ref.dtype)


def _sb_attention(sbqkv):
    b, s, _ = sbqkv.shape
    tq = min(TQ_ATTN, s)
    npair = SB_WIDTH // LANES
    return pl.pallas_call(
        functools.partial(_sb_body, tq=tq),
        grid=(b, npair, s // tq),
        in_specs=[pl.BlockSpec((1, tq, LANES), lambda bi, hp, qi: (bi, qi, hp)),
                  pl.BlockSpec((1, s, LANES), lambda bi, hp, qi: (bi, 0, npair + hp)),
                  pl.BlockSpec((1, s, LANES), lambda bi, hp, qi: (bi, 0, 2 * npair + hp))],
        out_specs=pl.BlockSpec((1, tq, LANES), lambda bi, hp, qi: (bi, qi, hp)),
        out_shape=jax.ShapeDtypeStruct((b, s, SB_WIDTH), BF16),
        compiler_params=_cparams(("parallel", "parallel", "arbitrary")),
        name="sb_attention",
    )(sbqkv, sbqkv, sbqkv)


def _split3(x):
    p1 = x.astype(BF16)
    r1 = x - p1.astype(F32)
    p2 = r1.astype(BF16)
    p3 = (r1 - p2.astype(F32)).astype(BF16)
    return p1, p2, p3


def _mlstm_body(qk_ref, vo_ref, if_ref, cw_ref, cb_ref, og_ref, o_ref, ext_ref, c_ref, n_ref, m_ref):
    L = ML_CHUNK
    hd = ML_HEAD_DIM
    tail = 8

    @pl.when(pl.program_id(1) == 0)
    def _():
        ext_ref[0:tail, :] = jnp.zeros((tail, 2 * ML_WIDTH), F32)
        c_ref[...] = jnp.zeros(c_ref.shape, F32)
        n_ref[...] = jnp.zeros(n_ref.shape, F32)
        m_ref[...] = jnp.zeros(m_ref.shape, F32)

    raw = qk_ref[0].astype(F32)
    ext_ref[tail:tail + L, :] = raw
    conv = cb_ref[...] + cw_ref[ML_CONV - 1:ML_CONV, :] * raw
    for j in range(1, ML_CONV):
        conv += cw_ref[ML_CONV - 1 - j:ML_CONV - j, :] * ext_ref[tail - j:tail - j + L, :]
    ext_ref[0:tail, :] = ext_ref[L:L + tail, :]
    qk = conv * jax.nn.sigmoid(conv)

    gl = lax.broadcasted_iota(jnp.int32, (1, LANES), 1)
    gin = if_ref[0]
    gates = jnp.where(gl < ML_HEADS, gin, jax.nn.log_sigmoid(gin))
    rr = lax.broadcasted_iota(jnp.int32, (L, L), 0)
    cc = lax.broadcasted_iota(jnp.int32, (L, L), 1)
    causal = rr >= cc
    tril = jnp.where(causal, 1.0, 0.0).astype(BF16)
    cum = sum(_dot(tril, p) for p in _split3(gates))
    gates_t = gates.T
    cum_t = cum.T

    for h in range(ML_HEADS):
        qf = qk[:, h * hd:(h + 1) * hd]
        qh = qf.astype(BF16)
        kf =qk[:, ML_WIDTH + h * hd:ML_WIDTH + (h + 1) * hd] * (hd ** -0.5)
        kh = kf.astype(BF16)
        vh = vo_ref[0, :, h * hd:(h + 1) * hd]
        c_prev = c_ref[h]
        n_prev = n_ref[h:h + 1, :]
        m_prev = m_ref[h:h + 1, 0:1]

        cf_col = cum[:, ML_HEADS + h:ML_HEADS + h + 1]
        ig_col = gates[:, h:h + 1]
        cf_row = cum_t[ML_HEADS + h:ML_HEADS + h + 1, :]
        ig_row = gates_t[h:h + 1, :]

        log_intra = jnp.where(causal, cf_col - cf_row + ig_row, -jnp.inf)
        log_inter = cf_col + m_prev
        m = jnp.maximum(log_inter, jnp.max(log_intra, axis=-1, keepdims=True))
        w_intra = jnp.exp(log_intra - m)
        w_inter = jnp.exp(log_inter - m)
        scores = _dot_nt(qh, kh) * w_intra
        num = _dot(scores.astype(BF16), vh) + w_inter * _dot(qh, c_prev.astype(BF16))
        qn = jnp.sum(qf * n_prev, axis=-1, keepdims=True)
        den = jnp.sum(scores, axis=-1, keepdims=True) + w_inter * qn
        hout = num / jnp.maximum(jnp.abs(den), jnp.exp(-m))

        f_total = cf_col[L - 1:L, :]
        log_to_end = f_total - cf_col + ig_col
        m_new = jnp.maximum(f_total + m_prev, jnp.max(log_to_end, axis=0, keepdims=True))
        decay = jnp.exp(f_total + m_prev - m_new)
        w_end = jnp.exp(log_to_end - m_new)
        kw = kf * w_end
        c_ref[h] = decay * c_prev + _dot_tn(kw.astype(BF16), vh)
        n_ref[h:h + 1, :] = decay * n_prev + jnp.sum(kw, axis=0, keepdims=True)
        m_ref[h:h + 1, :] = jnp.broadcast_to(m_new, (1, LANES))

        hn = _rms(hout, og_ref[:, h * hd:(h + 1) * hd])
        o_pre = vo_ref[0, :, ML_WIDTH + h * hd:ML_WIDTH + (h + 1) * hd].astype(F32)
        o_ref[0, :, h * hd:(h + 1) * hd] = (jax.nn.sigmoid(o_pre) * hn).astype(o_ref.dtype)


def _mlstm(mlqk, mlvo, gif, conv_w, conv_b, out_gain):
    b, s, _ = mlqk.shape
    L = ML_CHUNK
    blk = lambda w: pl.BlockSpec((1, L, w), lambda bi, ci: (bi, ci, 0))
    return pl.pallas_call(
        _mlstm_body,
        grid=(b, s // L),
        in_specs=[blk(W_MLQK), blk(W_MLVO), blk(W_IF),
                  _const_spec(conv_w.shape), _const_spec(conv_b.shape), _const_spec(out_gain.shape)],
        out_specs=blk(ML_WIDTH),
        out_shape=jax.ShapeDtypeStruct((b, s, ML_WIDTH), BF16),
        scratch_shapes=[pltpu.VMEM((L + 8, W_MLQK), F32),
                        pltpu.VMEM((ML_HEADS, ML_HEAD_DIM, ML_HEAD_DIM), F32),
                        pltpu.VMEM((8, ML_HEAD_DIM), F32),
                        pltpu.VMEM((8, LANES), F32)],
        compiler_params=_cparams(("parallel", "arbitrary")),
        name="mlstm",
    )(mlqk, mlvo, gif, conv_w, conv_b, out_gain)


def _rope_table_body(pos_ref, invf_ref, sign_ref, cos_ref, sin_ref):
    ang = pos_ref[...].astype(F32) * invf_ref[...]
    cos_ref[...] = jnp.cos(ang)
    sin_ref[...] = jnp.sin(ang) * sign_ref[...]


def _rope_tables(positions):
    t = positions.size
    half = MLA_ROPE // 2
    inv_freq = jnp.power(ROPE_THETA, -jnp.arange(half, dtype=F32) / half)
    zeros = jnp.zeros((half,), F32)
    pad = jnp.zeros((LANES - MLA_QK,), F32)
    nope = jnp.zeros((MLA_NOPE,), F32)
    invf = jnp.concatenate([nope, inv_freq, inv_freq, pad])[None, :]
    sign = jnp.concatenate([nope, zeros - 1.0, zeros + 1.0, pad])[None, :]
    tm = min(TM_DENSE, t)
    return pl.pallas_call(
        _rope_table_body,
        grid=(t // tm,),
        in_specs=[pl.BlockSpec((tm, 1), lambda i: (i, 0)), _const_spec(invf.shape), _const_spec(sign.shape)],
        out_specs=[pl.BlockSpec((tm, LANES), lambda i: (i, 0))] * 2,
        out_shape=[jax.ShapeDtypeStruct((t, LANES), F32)] * 2,
        compiler_params=_cparams(("parallel",)),
        name="rope_tables",
    )(positions.reshape(t, 1), invf, sign)


def _mla_prep_body(c_ref, cos_ref, sin_ref, qn_ref, kvn_ref, wq_ref, wk_ref, wv_ref, qg_ref, kg_ref,
                   q_out, k_out, v_out):
    c = c_ref[...].astype(F32)
    cq = _rms(c[:, :MLA_Q_RANK], qn_ref[...]).astype(BF16)
    ckv = _rms(c[:, MLA_Q_RANK:MLA_Q_RANK + MLA_KV_RANK], kvn_ref[...]).astype(BF16)
    k_rope = c[:, MLA_Q_RANK + MLA_KV_RANK:]
    q = _dot(cq, wq_ref[...])
    k = _dot(ckv, wk_ref[...])
    v_out[...] = _dot(ckv, wv_ref[...]).astype(v_out.dtype)
    cos = cos_ref[...]
    sin = sin_ref[...]
    lane = lax.broadcasted_iota(jnp.int32, (1, LANES), 1)
    first_half = lane < MLA_NOPE + MLA_ROPE // 2

    def norm_rope(x, gain):
        y = x * lax.rsqrt(jnp.sum(x * x, axis=-1, keepdims=True) * (1.0 / MLA_QK) + EPS) * gain
        rot = jnp.where(first_half, pltpu.roll(y, LANES - MLA_ROPE // 2, 1), pltpu.roll(y, MLA_ROPE // 2, 1))
        return y * cos + rot * sin

    for h in range(MLA_HEADS):
        sl = slice(h * LANES, (h + 1) * LANES)
        q_out[:, sl] = norm_rope(q[:, sl], qg_ref[...]).astype(q_out.dtype)
        k_out[:, sl] = norm_rope(k[:, sl] + k_rope, kg_ref[...]).astype(k_out.dtype)


def _mla_prep(mla_c, cos, sin, qn, kvn, wq, wk, wv, qg, kg):
    t = mla_c.shape[0]
    tm = min(TM_DENSE, t)
    row = lambda w: pl.BlockSpec((tm, w), lambda i: (i, 0))
    consts = (qn, kvn, wq, wk, wv, qg, kg)
    widths = (MLA_HEADS * LANES, MLA_HEADS * LANES, MLA_WIDTH)
    return pl.pallas_call(
        _mla_prep_body,
        grid=(t // tm,),
        in_specs=[row(W_MLA), row(LANES), row(LANES)] + [_const_spec(c.shape) for c in consts],
        out_specs=[row(w) for w in widths],
        out_shape=[jax.ShapeDtypeStruct((t, w), BF16) for w in widths],
        compiler_params=_cparams(("parallel",)),
        name="mla_prep",
    )(mla_c, cos, sin, *consts)


def _mla_attn_body(q_ref, k_ref, v_ref, o_ref, *, tq):
    qi = pl.program_id(2)
    lane = lax.broadcasted_iota(jnp.int32, (1, LANES), 1)
    qpos = lax.broadcasted_iota(jnp.int32, (tq, tq), 0)
    kpos = lax.broadcasted_iota(jnp.int32, (tq, tq), 1)
    outs = []
    for h in range(2):
        q = q_ref[0, :, h * LANES:(h + 1) * LANES]

        def step(kb, carry, masked, q=q, h=h):
            m, l, acc = carry
            start = pl.multiple_of(kb * tq, tq)
            k = k_ref[0, pl.ds(start, tq), h * LANES:(h + 1) * LANES]
            v = v_ref[0, pl.ds(start, tq), :]
            s = _dot_nt(q, k)
            if masked:
                s = jnp.where(kpos <= qpos, s, -jnp.inf)
            m_new = jnp.maximum(m, jnp.max(s, axis=-1, keepdims=True))
            alpha = jnp.exp(m - m_new)
            p = jnp.exp(s - m_new)
            l = alpha * l + jnp.sum(p, axis=-1, keepdims=True)
            acc = alpha * acc + _dot(p.astype(BF16), v)
            return m_new, l, acc

        init = (jnp.full((tq, 1), -jnp.inf, F32), jnp.zeros((tq, 1), F32), jnp.zeros((tq, LANES), F32))
        carry = lax.fori_loop(0, qi, functools.partial(step, masked=False), init)
        _, l, acc = step(qi, carry, True)
        outs.append(acc / l)
    o_ref[0] = jnp.where(lane < MLA_V, outs[0], outs[1]).astype(o_ref.dtype)


def _mla_attention(q, k, v):
    b, s, _ = q.shape
    tq = min(TQ_ATTN, s)
    npair = MLA_HEADS // 2
    return pl.pallas_call(
        functools.partial(_mla_attn_body, tq=tq),
        grid=(b, npair, s // tq),
        in_specs=[pl.BlockSpec((1, tq, 2 * LANES), lambda bi, hp, qi: (bi, qi, hp)),
                  pl.BlockSpec((1, s, 2 * LANES), lambda bi, hp, qi: (bi, 0, hp)),
                  pl.BlockSpec((1, s, LANES), lambda bi, hp, qi: (bi, 0, hp))],
        out_specs=pl.BlockSpec((1, tq, LANES), lambda bi, hp, qi: (bi, qi, hp)),
        out_shape=jax.ShapeDtypeStruct((b, s, MLA_WIDTH), BF16),
        compiler_params=_cparams(("parallel", "parallel", "arbitrary")),
        name="mla_attention",
    )(q, k, v)


def _seg(z, name):
    off = 0
    for n, w in SEGMENTS:
        if n == name:
            return z[..., off:off + w]
        off += w
    raise KeyError(name)


def _pack_inproj(w, b):
    wb = jnp.concatenate([w, b[None, :]], axis=0)
    z = lambda n: jnp.zeros((wb.shape[0], n), wb.dtype)
    kr = jnp.concatenate([z(MLA_NOPE), _seg(wb, 'mla_kr'), z(LANES - MLA_QK)], axis=1)
    gif = jnp.concatenate([_seg(wb, 'ml_i'), _seg(wb, 'ml_f'), z(LANES - 2 * ML_HEADS)], axis=1)
    packed = jnp.concatenate(
        [_seg(wb, 'gates'), _seg(wb, 'ml_q'), _seg(wb, 'ml_k'), _seg(wb, 'ml_v'), _seg(wb, 'ml_o'),
         _seg(wb, 'sb_q'), _seg(wb, 'sb_k'), _seg(wb, 'sb_v'), _seg(wb, 'mla_cq'), _seg(wb, 'mla_ckv'),
         kr, gif], axis=1)
    return packed[:-1].astype(BF16), packed[-1:]


def _pad_heads(w, width, keep):
    r = w.shape[0]
    wh = w.reshape(r, MLA_HEADS, width)[:, :, :keep]
    return jnp.pad(wh, ((0, 0), (0, 0), (0, LANES - keep))).reshape(r, MLA_HEADS * LANES)


def kernel(x, positions, ffn1_norm, ffn1_wi, ffn1_wo, mix_norm, w_in, b_in, ml_conv_w, ml_conv_b,
           ml_out_norm, mla_q_norm, mla_kv_norm, mla_wq_up, mla_wkv_up, mla_q_gain, mla_k_gain,
           w_up_sb, w_up_ml, w_up_mla, w_out, ffn2_norm, ffn2_wi, ffn2_wo):
    b, s, d = x.shape
    t = b * s
    depth = w_in.shape[0]
    cos, sin = _rope_tables(positions)
    xt = x.reshape(t, d)
    row = lambda a: a[None, :]
    pad_gain = lambda g: jnp.pad(g, (0, LANES - MLA_QK))[None, :]
    for l in range(depth):
        xt = _ffn(xt, row(ffn1_norm[l]), ffn1_wi[l].astype(BF16), ffn1_wo[l].astype(BF16))

        w_packed, b_packed = _pack_inproj(w_in[l], b_in[l])
        gates, mlqk, mlvo, sbqkv, mla_c, gif = _inproj(xt, row(mix_norm[l]), w_packed, b_packed)

        ysb = _sb_attention(sbqkv.reshape(b, s, W_SB))

        yml = _mlstm(mlqk.reshape(b, s, W_MLQK), mlvo.reshape(b, s, W_MLVO), gif.reshape(b, s, W_IF),
                     ml_conv_w[l], row(ml_conv_b[l]), row(ml_out_norm[l]))

        wkv = mla_wkv_up[l]
        wv = wkv.reshape(MLA_KV_RANK, MLA_HEADS, MLA_NOPE + MLA_V)[:, :, MLA_NOPE:].reshape(MLA_KV_RANK, MLA_WIDTH)
        qh, kh, vh = _mla_prep(
            mla_c, cos, sin, row(mla_q_norm[l]), row(mla_kv_norm[l]),
            _pad_heads(mla_wq_up[l], MLA_QK, MLA_QK).astype(BF16),
            _pad_heads(wkv, MLA_NOPE + MLA_V, MLA_NOPE).astype(BF16), wv.astype(BF16),
            pad_gain(mla_q_gain[l] * MLA_QK ** -0.5), pad_gain(mla_k_gain[l]))
        ymla = _mla_attention(qh.reshape(b, s, -1), kh.reshape(b, s, -1), vh.reshape(b, s, -1))

        xt = _merge_ffn(xt, gates, ysb.reshape(t, -1), yml.reshape(t, -1), ymla.reshape(t, -1),
                        w_up_sb[l].astype(BF16), w_up_ml[l].astype(BF16), w_up_mla[l].astype(BF16),
                        w_out[l].astype(BF16), row(ffn2_norm[l]), ffn2_wi[l].astype(BF16), ffn2_wo[l].astype(BF16))
    return xt.reshape(b, s, d)
```

```python
import functools

import numpy as np
import jax
import jax.numpy as jnp
from jax import lax
from jax.experimental import pallas as pl
from jax.experimental.pallas import tpu as pltpu

F32 = jnp.float32
BF16 = jnp.bfloat16

EPS = 1e-6
LANES = 128
V7X_VMEM_LIMIT = 56 * 1024 * 1024

D_MODEL = 1024
FFN_DIM = 1408
SB_HEADS, SB_HEAD_DIM = 4, 64
SB_WIDTH = SB_HEADS * SB_HEAD_DIM
ML_HEADS, ML_HEAD_DIM = 4, 128
ML_WIDTH = ML_HEADS * ML_HEAD_DIM
ML_CHUNK = 128
ML_CONV = 4
MLA_HEADS, MLA_NOPE, MLA_ROPE, MLA_V = 4, 64, 32, 64
MLA_QK = MLA_NOPE + MLA_ROPE
MLA_WIDTH = MLA_HEADS * MLA_V
MLA_Q_RANK, MLA_KV_RANK = 256, 128
ROPE_THETA = 10000.0
N_BRANCH = 3

SEGMENTS = (
    ('sb_q', SB_WIDTH), ('sb_k', SB_WIDTH), ('sb_v', SB_WIDTH),
    ('ml_q', ML_WIDTH), ('ml_k', ML_WIDTH), ('ml_v', ML_WIDTH), ('ml_o', ML_WIDTH),
    ('ml_i', ML_HEADS), ('ml_f', ML_HEADS),
    ('mla_cq', MLA_Q_RANK), ('mla_ckv', MLA_KV_RANK), ('mla_kr', MLA_ROPE),
    ('gates', N_BRANCH * D_MODEL),
)

W_GATES = N_BRANCH * D_MODEL
W_MLQK = 2 * ML_WIDTH
W_MLVO = 2 * ML_WIDTH
W_SB = 3 * SB_WIDTH
W_MLA = MLA_Q_RANK + MLA_KV_RANK + LANES
W_IF = LANES
OUT_WIDTHS = (W_GATES, W_MLQK, W_MLVO, W_SB, W_MLA, W_IF)
N_PACKED = sum(OUT_WIDTHS)

TM_DENSE = 512
TQ_ATTN = 256


def _cparams(sem):
    return pltpu.CompilerParams(dimension_semantics=sem, vmem_limit_bytes=V7X_VMEM_LIMIT)


def _const_spec(shape):
    nd = len(shape)
    return pl.BlockSpec(shape, lambda *_: (0,) * nd)


def _rms(x, gain):
    return x * lax.rsqrt(jnp.mean(x * x, axis=-1, keepdims=True) + EPS) * gain


def _dot(a, b):
    return jnp.dot(a, b, preferred_element_type=F32)


def _dot_nt(a, b):
    return lax.dot_general(a, b, (((1,), (1,)), ((), ())), preferred_element_type=F32)


def _dot_tn(a, b):
    return lax.dot_general(a, b, (((0,), (0,)), ((), ())), preferred_element_type=F32)


def _swiglu_residual(x, gain, wi_ref, wo_ref):
    u = _rms(x, gain).astype(BF16)
    h = _dot(u, wi_ref[...])
    a = h[:, :FFN_DIM]
    g = h[:, FFN_DIM:]
    act = (a * jax.nn.sigmoid(a) * g).astype(BF16)
    return x + 0.5 * _dot(act, wo_ref[...])


def _ffn_body(x_ref, g_ref, wi_ref, wo_ref, o_ref):
    o_ref[...] = _swiglu_residual(x_ref[...], g_ref[...], wi_ref, wo_ref)


def _ffn(x, gain, wi, wo):
    t, d = x.shape
    tm = min(TM_DENSE, t)
    return pl.pallas_call(
        _ffn_body,
        grid=(t // tm,),
        in_specs=[pl.BlockSpec((tm, d), lambda i: (i, 0)),
                  _const_spec(gain.shape), _const_spec(wi.shape), _const_spec(wo.shape)],
        out_specs=pl.BlockSpec((tm, d), lambda i: (i, 0)),
        out_shape=jax.ShapeDtypeStruct((t, d), F32),
        compiler_params=_cparams(("parallel",)),
        name="ffn",
    )(x, gain, wi, wo)


def _inproj_body(x_ref, g_ref, w_ref, b_ref, *out_refs):
    u = _rms(x_ref[...], g_ref[...]).astype(BF16)
    off = 0
    for o_ref, width in zip(out_refs, OUT_WIDTHS):
        step = min(width, 1024)
        for c in range(0, width, step):
            z = _dot(u, w_ref[:, off + c:off + c + step]) + b_ref[:, off + c:off + c + step]
            o_ref[:, c:c + step] = z.astype(o_ref.dtype)
        off += width


def _inproj(x, gain, w, b):
    t, d = x.shape
    tm = min(TM_DENSE, t)
    dtypes = (BF16, BF16, BF16, BF16, BF16, F32)
    return pl.pallas_call(
        _inproj_body,
        grid=(t // tm,),
        in_specs=[pl.BlockSpec((tm, d), lambda i: (i, 0)),
                  _const_spec(gain.shape), _const_spec(w.shape), _const_spec(b.shape)],
        out_specs=[pl.BlockSpec((tm, wd), lambda i: (i, 0)) for wd in OUT_WIDTHS],
        out_shape=[jax.ShapeDtypeStruct((t, wd), dt) for wd, dt in zip(OUT_WIDTHS, dtypes)],
        compiler_params=_cparams(("parallel",)),
        name="inproj",
    )(x, gain, w, b)


def _merge_body(x_ref, gt_ref, ysb_ref, yml_ref, ymla_ref, wsb_ref, wml_ref, wmla_ref, wout_ref,
                g2_ref, wi_ref, wo_ref, o_ref):
    d = D_MODEL
    merged = jax.nn.sigmoid(gt_ref[:, 0:d].astype(F32)) * _dot(ysb_ref[...], wsb_ref[...])
    merged += jax.nn.sigmoid(gt_ref[:, d:2 * d].astype(F32)) * _dot(yml_ref[...], wml_ref[...])
    merged += jax.nn.sigmoid(gt_ref[:, 2 * d:3 * d].astype(F32)) * _dot(ymla_ref[...], wmla_ref[...])
    x = x_ref[...] + _dot(merged.astype(BF16), wout_ref[...])
    o_ref[...] = _swiglu_residual(x, g2_ref[...], wi_ref, wo_ref)


def _merge_ffn(x, gates, ysb, yml, ymla, wsb, wml, wmla, wout, g2, wi, wo):
    t, d = x.shape
    tm = min(TM_DENSE, t)
    row = lambda a: pl.BlockSpec((tm, a.shape[1]), lambda i: (i, 0))
    consts = (wsb, wml, wmla, wout, g2, wi, wo)
    return pl.pallas_call(
        _merge_body,
        grid=(t // tm,),
        in_specs=[row(x), row(gates), row(ysb), row(yml), row(ymla)] + [_const_spec(c.shape) for c in consts],
        out_specs=pl.BlockSpec((tm, d), lambda i: (i, 0)),
        out_shape=jax.ShapeDtypeStruct((t, d), F32),
        compiler_params=_cparams(("parallel",)),
        name="merge_ffn",
    )(x, gates, ysb, yml, ymla, *consts)


def _sb_body(q_ref, k_ref, v_ref, o_ref, qm_ref, *, tq):
    qi = pl.program_id(1)
    lane = lax.broadcasted_iota(jnp.int32, (1, LANES), 1)
    qpos = lax.broadcasted_iota(jnp.int32, (tq, tq), 0)
    kpos = lax.broadcasted_iota(jnp.int32, (tq, tq), 1)
    tri = jnp.where(qpos > kpos, 1.0, 0.0).astype(BF16)
    strict = kpos < qpos

    scale = SB_HEAD_DIM ** -0.5
    for h in range(SB_HEADS):
        qp = q_ref[0, :, (h // 2) * LANES:(h // 2 + 1) * LANES]
        lo = (h % 2) * SB_HEAD_DIM
        qm_ref[h] = jnp.where((lane >= lo) & (lane < lo + SB_HEAD_DIM), qp, 0) * scale

    def sweep(kb, carry, masked):
        start = pl.multiple_of(kb * tq, tq)
        new = []
        for h in range(SB_HEADS):
            pair = slice((h // 2) * LANES, (h // 2 + 1) * LANES)
            run, acc = carry[h]
            s = _dot_nt(qm_ref[h], k_ref[0, pl.ds(start, tq), pair])
            ls = jnp.minimum(s, 0.0) - jnp.log(1.0 + jnp.exp(-jnp.abs(s)))
            lk = ls - s
            if masked:
                lk = jnp.where(strict, lk, 0.0)
            w = jnp.exp(ls + _dot(lk.astype(BF16), tri) + run)
            if masked:
                w = jnp.where(strict, w, 0.0)
            acc = acc + _dot(w.astype(BF16), v_ref[0, pl.ds(start, tq), pair])
            run = run + jnp.sum(lk, axis=-1, keepdims=True)
            new.append((run, acc))
        return tuple(new)

    init = tuple((jnp.zeros((tq, 1), F32), jnp.zeros((tq, LANES), F32)) for _ in range(SB_HEADS))
    carry = sweep(qi, init, True)
    carry = lax.fori_loop(0, qi, lambda i, c: sweep(qi - 1 - i, c, False), carry)
    for p in range(SB_HEADS // 2):
        o_ref[0, :, p * LANES:(p + 1) * LANES] = jnp.where(
            lane < SB_HEAD_DIM, carry[2 * p][1], carry[2 * p + 1][1]).astype(o_ref.dtype)


def _sb_attention(sbqkv):
    b, s, _ = sbqkv.shape
    tq = min(TQ_ATTN, s)
    return pl.pallas_call(
        functools.partial(_sb_body, tq=tq),
        grid=(b, s // tq),
        in_specs=[pl.BlockSpec((1, tq, SB_WIDTH), lambda bi, qi: (bi, qi, 0)),
                  pl.BlockSpec((1, s, SB_WIDTH), lambda bi, qi: (bi, 0, 1)),
                  pl.BlockSpec((1, s, SB_WIDTH), lambda bi, qi: (bi, 0, 2))],
        out_specs=pl.BlockSpec((1, tq, SB_WIDTH), lambda bi, qi: (bi, qi, 0)),
        out_shape=jax.ShapeDtypeStruct((b, s, SB_WIDTH), BF16),
        scratch_shapes=[pltpu.VMEM((SB_HEADS, tq, LANES), BF16)],
        compiler_params=_cparams(("parallel", "arbitrary")),
        name="sb_attention",
    )(sbqkv, sbqkv, sbqkv)


def _split3(x):
    p1 = x.astype(BF16)
    r1 = x - p1.astype(F32)
    p2 = r1.astype(BF16)
    p3 = (r1 - p2.astype(F32)).astype(BF16)
    return p1, p2, p3


def _mlstm_body(qk_ref, vo_ref, if_ref, cw_ref, cb_ref, og_ref, o_ref, ext_ref, c_ref, n_ref, m_ref):
    L = ML_CHUNK
    hd = ML_HEAD_DIM
    tail = 8

    @pl.when(pl.program_id(1) == 0)
    def _():
        ext_ref[0:tail, :] = jnp.zeros((tail, 2 * ML_WIDTH), F32)
        c_ref[...] = jnp.zeros(c_ref.shape, F32)
        n_ref[...] = jnp.zeros(n_ref.shape, F32)
        m_ref[...] = jnp.zeros(m_ref.shape, F32)

    raw = qk_ref[0].astype(F32)
    ext_ref[tail:tail + L, :] = raw
    conv = cb_ref[...] + cw_ref[ML_CONV - 1:ML_CONV, :] * raw
    for j in range(1, ML_CONV):
        conv += cw_ref[ML_CONV - 1 - j:ML_CONV - j, :] * ext_ref[tail - j:tail - j + L, :]
    ext_ref[0:tail, :] = ext_ref[L:L + tail, :]
    qk = conv * jax.nn.sigmoid(conv)

    gl = lax.broadcasted_iota(jnp.int32, (1, LANES), 1)
    gin = if_ref[0]
    gates = jnp.where(gl < ML_HEADS, gin, jax.nn.log_sigmoid(gin))
    rr = lax.broadcasted_iota(jnp.int32, (L, L), 0)
    cc = lax.broadcasted_iota(jnp.int32, (L, L), 1)
    causal = rr >= cc
    tril = jnp.where(causal, 1.0, 0.0).astype(BF16)
    cum = sum(_dot(tril, p) for p in _split3(gates))
    gates_t = gates.T
    cum_t = cum.T

    for h in range(ML_HEADS):
        qf = qk[:, h * hd:(h + 1) * hd]
        qh = qf.astype(BF16)
        kf =qk[:, ML_WIDTH + h * hd:ML_WIDTH + (h + 1) * hd] * (hd ** -0.5)
        kh = kf.astype(BF16)
        vh = vo_ref[0, :, h * hd:(h + 1) * hd]
        c_prev = c_ref[h]
        n_prev = n_ref[h:h + 1, :]
        m_prev = m_ref[h:h + 1, 0:1]

        cf_col = cum[:, ML_HEADS + h:ML_HEADS + h + 1]
        ig_col = gates[:, h:h + 1]
        cf_row = cum_t[ML_HEADS + h:ML_HEADS + h + 1, :]
        ig_row = gates_t[h:h + 1, :]

        log_intra = jnp.where(causal, cf_col - cf_row + ig_row, -jnp.inf)
        log_inter = cf_col + m_prev
        m = jnp.maximum(log_inter, jnp.max(log_intra, axis=-1, keepdims=True))
        w_intra = jnp.exp(log_intra - m)
        w_inter = jnp.exp(log_inter - m)
        scores = _dot_nt(qh, kh) * w_intra
        num = _dot(scores.astype(BF16), vh) + w_inter * _dot(qh, c_prev.astype(BF16))
        qn = jnp.sum(qf * n_prev, axis=-1, keepdims=True)
        den = jnp.sum(scores, axis=-1, keepdims=True) + w_inter * qn
        hout = num / jnp.maximum(jnp.abs(den), jnp.exp(-m))

        f_total = cf_col[L - 1:L, :]
        log_to_end = f_total - cf_col + ig_col
        m_new = jnp.maximum(f_total + m_prev, jnp.max(log_to_end, axis=0, keepdims=True))
        decay = jnp.exp(f_total + m_prev - m_new)
        w_end = jnp.exp(log_to_end - m_new)
        kw = kf * w_end
        c_ref[h] = decay * c_prev + _dot_tn(kw.astype(BF16), vh)
        n_ref[h:h + 1, :] = decay * n_prev + jnp.sum(kw, axis=0, keepdims=True)
        m_ref[h:h + 1, :] = jnp.broadcast_to(m_new, (1, LANES))

        hn = _rms(hout, og_ref[:, h * hd:(h + 1) * hd])
        o_pre = vo_ref[0, :, ML_WIDTH + h * hd:ML_WIDTH + (h + 1) * hd].astype(F32)
        o_ref[0, :, h * hd:(h + 1) * hd] = (jax.nn.sigmoid(o_pre) * hn).astype(o_ref.dtype)


def _mlstm(mlqk, mlvo, gif, conv_w, conv_b, out_gain):
    b, s, _ = mlqk.shape
    L = ML_CHUNK
    blk = lambda w: pl.BlockSpec((1, L, w), lambda bi, ci: (bi, ci, 0))
    return pl.pallas_call(
        _mlstm_body,
        grid=(b, s // L),
        in_specs=[blk(W_MLQK), blk(W_MLVO), blk(W_IF),
                  _const_spec(conv_w.shape), _const_spec(conv_b.shape), _const_spec(out_gain.shape)],
        out_specs=blk(ML_WIDTH),
        out_shape=jax.ShapeDtypeStruct((b, s, ML_WIDTH), BF16),
        scratch_shapes=[pltpu.VMEM((L + 8, W_MLQK), F32),
                        pltpu.VMEM((ML_HEADS, ML_HEAD_DIM, ML_HEAD_DIM), F32),
                        pltpu.VMEM((8, ML_HEAD_DIM), F32),
                        pltpu.VMEM((8, LANES), F32)],
        compiler_params=_cparams(("parallel", "arbitrary")),
        name="mlstm",
    )(mlqk, mlvo, gif, conv_w, conv_b, out_gain)


def _rope_table_body(pos_ref, invf_ref, sign_ref, cos_ref, sin_ref):
    ang = pos_ref[...].astype(F32) * invf_ref[...]
    cos_ref[...] = jnp.cos(ang)
    sin_ref[...] = jnp.sin(ang) * sign_ref[...]


def _rope_tables(positions):
    t = positions.size
    half = MLA_ROPE // 2
    inv_freq = jnp.power(ROPE_THETA, -jnp.arange(half, dtype=F32) / half)
    zeros = jnp.zeros((half,), F32)
    pad = jnp.zeros((LANES - MLA_QK,), F32)
    nope = jnp.zeros((MLA_NOPE,), F32)
    invf = jnp.concatenate([nope, inv_freq, inv_freq, pad])[None, :]
    sign = jnp.concatenate([nope, zeros - 1.0, zeros + 1.0, pad])[None, :]
    tm = min(TM_DENSE, t)
    return pl.pallas_call(
        _rope_table_body,
        grid=(t // tm,),
        in_specs=[pl.BlockSpec((tm, 1), lambda i: (i, 0)), _const_spec(invf.shape), _const_spec(sign.shape)],
        out_specs=[pl.BlockSpec((tm, LANES), lambda i: (i, 0))] * 2,
        out_shape=[jax.ShapeDtypeStruct((t, LANES), F32)] * 2,
        compiler_params=_cparams(("parallel",)),
        name="rope_tables",
    )(positions.reshape(t, 1), invf, sign)


def _mla_prep_body(c_ref, cos_ref, sin_ref, qn_ref, kvn_ref, wq_ref, wk_ref, wv_ref, qg_ref, kg_ref,
                   q_out, k_out, v_out):
    c = c_ref[...].astype(F32)
    cq = _rms(c[:, :MLA_Q_RANK], qn_ref[...]).astype(BF16)
    ckv = _rms(c[:, MLA_Q_RANK:MLA_Q_RANK + MLA_KV_RANK], kvn_ref[...]).astype(BF16)
    k_rope = c[:, MLA_Q_RANK + MLA_KV_RANK:]
    q = _dot(cq, wq_ref[...])
    k = _dot(ckv, wk_ref[...])
    v_out[...] = _dot(ckv, wv_ref[...]).astype(v_out.dtype)
    cos = cos_ref[...]
    sin = sin_ref[...]
    lane = lax.broadcasted_iota(jnp.int32, (1, LANES), 1)
    first_half = lane < MLA_NOPE + MLA_ROPE // 2

    def norm_rope(x, gain):
        y = x * lax.rsqrt(jnp.sum(x * x, axis=-1, keepdims=True) * (1.0 / MLA_QK) + EPS) * gain
        rot = jnp.where(first_half, pltpu.roll(y, LANES - MLA_ROPE // 2, 1), pltpu.roll(y, MLA_ROPE // 2, 1))
        return y * cos + rot * sin

    for h in range(MLA_HEADS):
        sl = slice(h * LANES, (h + 1) * LANES)
        q_out[:, sl] = norm_rope(q[:, sl], qg_ref[...]).astype(q_out.dtype)
        k_out[:, sl] = norm_rope(k[:, sl] + k_rope, kg_ref[...]).astype(k_out.dtype)


def _mla_prep(mla_c, cos, sin, qn, kvn, wq, wk, wv, qg, kg):
    t = mla_c.shape[0]
    tm = min(TM_DENSE, t)
    row = lambda w: pl.BlockSpec((tm, w), lambda i: (i, 0))
    consts = (qn, kvn, wq, wk, wv, qg, kg)
    widths = (MLA_HEADS * LANES, MLA_HEADS * LANES, MLA_WIDTH)
    return pl.pallas_call(
        _mla_prep_body,
        grid=(t // tm,),
        in_specs=[row(W_MLA), row(LANES), row(LANES)] + [_const_spec(c.shape) for c in consts],
        out_specs=[row(w) for w in widths],
        out_shape=[jax.ShapeDtypeStruct((t, w), BF16) for w in widths],
        compiler_params=_cparams(("parallel",)),
        name="mla_prep",
    )(mla_c, cos, sin, *consts)


def _mla_attn_body(q_ref, k_ref, v_ref, o_ref, *, tq):
    qi = pl.program_id(1)
    lane = lax.broadcasted_iota(jnp.int32, (1, LANES), 1)
    qpos = lax.broadcasted_iota(jnp.int32, (tq, tq), 0)
    kpos = lax.broadcasted_iota(jnp.int32, (tq, tq), 1)
    causal = kpos <= qpos
    ones = jnp.ones((tq, LANES), BF16)

    def step(kb, carry, masked):
        start = pl.multiple_of(kb * tq, tq)
        new = []
        for h in range(MLA_HEADS):
            m, acc = carry[h]
            pair = slice((h // 2) * LANES, (h // 2 + 1) * LANES)
            q = q_ref[0, :, h * LANES:(h + 1) * LANES]
            k = k_ref[0, pl.ds(start, tq), h * LANES:(h + 1) * LANES]
            v_ext = jnp.concatenate([v_ref[0, pl.ds(start, tq), pair], ones], axis=1)
            s = _dot_nt(q, k)
            if masked:
                s = jnp.where(causal, s, -jnp.inf)
            m_new = jnp.maximum(m, jnp.max(s, axis=-1, keepdims=True))
            acc = jnp.exp(m - m_new) * acc + _dot(jnp.exp(s - m_new).astype(BF16), v_ext)
            new.append((m_new, acc))
        return tuple(new)

    init = tuple((jnp.full((tq, 1), -jnp.inf, F32), jnp.zeros((tq, 2 * LANES), F32)) for _ in range(MLA_HEADS))
    carry = lax.fori_loop(0, qi, functools.partial(step, masked=False), init)
    carry = step(qi, carry, True)
    outs = [acc[:, :LANES] / acc[:, LANES:] for _, acc in carry]
    for p in range(MLA_HEADS // 2):
        o_ref[0, :, p * LANES:(p + 1) * LANES] = jnp.where(
            lane < MLA_V, outs[2 * p], outs[2 * p + 1]).astype(o_ref.dtype)


def _mla_attention(q, k, v):
    b, s, _ = q.shape
    tq = min(TQ_ATTN, s)
    tile = lambda w: pl.BlockSpec((1, tq, w), lambda bi, qi: (bi, qi, 0))
    full = lambda w: pl.BlockSpec((1, s, w), lambda bi, qi: (bi, 0, 0))
    return pl.pallas_call(
        functools.partial(_mla_attn_body, tq=tq),
        grid=(b, s // tq),
        in_specs=[tile(MLA_HEADS * LANES), full(MLA_HEADS * LANES), full(MLA_WIDTH)],
        out_specs=tile(MLA_WIDTH),
        out_shape=jax.ShapeDtypeStruct((b, s, MLA_WIDTH), BF16),
        compiler_params=_cparams(("parallel", "arbitrary")),
        name="mla_attention",
    )(q, k, v)


def _seg(z, name):
    off = 0
    for n, w in SEGMENTS:
        if n == name:
            return z[..., off:off + w]
        off += w
    raise KeyError(name)


def _pack_inproj(w, b):
    wb = jnp.concatenate([w, b[None, :]], axis=0)
    z = lambda n: jnp.zeros((wb.shape[0], n), wb.dtype)
    kr = jnp.concatenate([z(MLA_NOPE), _seg(wb, 'mla_kr'), z(LANES - MLA_QK)], axis=1)
    gif = jnp.concatenate([_seg(wb, 'ml_i'), _seg(wb, 'ml_f'), z(LANES - 2 * ML_HEADS)], axis=1)
    packed = jnp.concatenate(
        [_seg(wb, 'gates'), _seg(wb, 'ml_q'), _seg(wb, 'ml_k'), _seg(wb, 'ml_v'), _seg(wb, 'ml_o'),
         _seg(wb, 'sb_q'), _seg(wb, 'sb_k'), _seg(wb, 'sb_v'), _seg(wb, 'mla_cq'), _seg(wb, 'mla_ckv'),
         kr, gif], axis=1)
    return packed[:-1].astype(BF16), packed[-1:]


def _pad_heads(w, width, keep):
    r = w.shape[0]
    wh = w.reshape(r, MLA_HEADS, width)[:, :, :keep]
    return jnp.pad(wh, ((0, 0), (0, 0), (0, LANES - keep))).reshape(r, MLA_HEADS * LANES)


def kernel(x, positions, ffn1_norm, ffn1_wi, ffn1_wo, mix_norm, w_in, b_in, ml_conv_w, ml_conv_b,
           ml_out_norm, mla_q_norm, mla_kv_norm, mla_wq_up, mla_wkv_up, mla_q_gain, mla_k_gain,
           w_up_sb, w_up_ml, w_up_mla, w_out, ffn2_norm, ffn2_wi, ffn2_wo):
    b, s, d = x.shape
    t = b * s
    depth = w_in.shape[0]
    cos, sin = _rope_tables(positions)
    xt = x.reshape(t, d)
    row = lambda a: a[None, :]
    pad_gain = lambda g: jnp.pad(g, (0, LANES - MLA_QK))[None, :]
    for l in range(depth):
        xt = _ffn(xt, row(ffn1_norm[l]), ffn1_wi[l].astype(BF16), ffn1_wo[l].astype(BF16))

        w_packed, b_packed = _pack_inproj(w_in[l], b_in[l])
        gates, mlqk, mlvo, sbqkv, mla_c, gif = _inproj(xt, row(mix_norm[l]), w_packed, b_packed)

        ysb = _sb_attention(sbqkv.reshape(b, s, W_SB))

        yml = _mlstm(mlqk.reshape(b, s, W_MLQK), mlvo.reshape(b, s, W_MLVO), gif.reshape(b, s, W_IF),
                     ml_conv_w[l], row(ml_conv_b[l]), row(ml_out_norm[l]))

        wkv = mla_wkv_up[l]
        wv = wkv.reshape(MLA_KV_RANK, MLA_HEADS, MLA_NOPE + MLA_V)[:, :, MLA_NOPE:].reshape(MLA_KV_RANK, MLA_WIDTH)
        qh, kh, vh = _mla_prep(
            mla_c, cos, sin, row(mla_q_norm[l]), row(mla_kv_norm[l]),
            _pad_heads(mla_wq_up[l], MLA_QK, MLA_QK).astype(BF16),
            _pad_heads(wkv, MLA_NOPE + MLA_V, MLA_NOPE).astype(BF16), wv.astype(BF16),
            pad_gain(mla_q_gain[l] * MLA_QK ** -0.5), pad_gain(mla_k_gain[l]))
        ymla = _mla_attention(qh.reshape(b, s, -1), kh.reshape(b, s, -1), vh.reshape(b, s, -1))

        xt = _merge_ffn(xt, gates, ysb.reshape(t, -1), yml.reshape(t, -1), ymla.reshape(t, -1),
                        w_up_sb[l].astype(BF16), w_up_ml[l].astype(BF16), w_up_mla[l].astype(BF16),
                        w_out[l].astype(BF16), row(ffn2_norm[l]), ffn2_wi[l].astype(BF16), ffn2_wo[l].astype(BF16))
    return xt.reshape(b, s, d)
```

```python
import functools

import numpy as np
import jax
import jax.numpy as jnp
from jax import lax
from jax.experimental import pallas as pl
from jax.experimental.pallas import tpu as pltpu

F32 = jnp.float32
BF16 = jnp.bfloat16

EPS = 1e-6
LANES = 128
V7X_VMEM_LIMIT = 56 * 1024 * 1024

D_MODEL = 1024
FFN_DIM = 1408
SB_HEADS, SB_HEAD_DIM = 4, 64
SB_WIDTH = SB_HEADS * SB_HEAD_DIM
ML_HEADS, ML_HEAD_DIM = 4, 128
ML_WIDTH = ML_HEADS * ML_HEAD_DIM
ML_CHUNK = 128
ML_CONV = 4
MLA_HEADS, MLA_NOPE, MLA_ROPE, MLA_V = 4, 64, 32, 64
MLA_QK = MLA_NOPE + MLA_ROPE
MLA_WIDTH = MLA_HEADS * MLA_V
MLA_Q_RANK, MLA_KV_RANK = 256, 128
ROPE_THETA = 10000.0
N_BRANCH = 3

SEGMENTS = (
    ('sb_q', SB_WIDTH), ('sb_k', SB_WIDTH), ('sb_v', SB_WIDTH),
    ('ml_q', ML_WIDTH), ('ml_k', ML_WIDTH), ('ml_v', ML_WIDTH), ('ml_o', ML_WIDTH),
    ('ml_i', ML_HEADS), ('ml_f', ML_HEADS),
    ('mla_cq', MLA_Q_RANK), ('mla_ckv', MLA_KV_RANK), ('mla_kr', MLA_ROPE),
    ('gates', N_BRANCH * D_MODEL),
)

W_GATES = N_BRANCH * D_MODEL
W_MLQK = 2 * ML_WIDTH
W_MLVO = 2 * ML_WIDTH
W_SB = 3 * SB_WIDTH
W_MLA = MLA_Q_RANK + MLA_KV_RANK + LANES
W_IF = LANES
OUT_WIDTHS = (W_GATES, W_MLQK, W_MLVO, W_SB, W_MLA, W_IF)
N_PACKED = sum(OUT_WIDTHS)

TM_DENSE = 512
TQ_ATTN = 256
SB_EXP_UNDERFLOW = -120.0


def _cparams(sem):
    return pltpu.CompilerParams(dimension_semantics=sem, vmem_limit_bytes=V7X_VMEM_LIMIT)


def _const_spec(shape):
    nd = len(shape)
    return pl.BlockSpec(shape, lambda *_: (0,) * nd)


def _rms(x, gain):
    return x * lax.rsqrt(jnp.mean(x * x, axis=-1, keepdims=True) + EPS) * gain


def _dot(a, b):
    return jnp.dot(a, b, preferred_element_type=F32)


def _dot_nt(a, b):
    return lax.dot_general(a, b, (((1,), (1,)), ((), ())), preferred_element_type=F32)


def _dot_tn(a, b):
    return lax.dot_general(a, b, (((0,), (0,)), ((), ())), preferred_element_type=F32)


def _swiglu_residual(x, gain, wi_ref, wo_ref):
    u = _rms(x, gain).astype(BF16)
    h = _dot(u, wi_ref[...])
    a = h[:, :FFN_DIM]
    g = h[:, FFN_DIM:]
    act = (a * jax.nn.sigmoid(a) * g).astype(BF16)
    return x + 0.5 * _dot(act, wo_ref[...])


def _ffn_body(x_ref, g_ref, wi_ref, wo_ref, o_ref):
    o_ref[...] = _swiglu_residual(x_ref[...], g_ref[...], wi_ref, wo_ref)


def _ffn(x, gain, wi, wo):
    t, d = x.shape
    tm = min(TM_DENSE, t)
    return pl.pallas_call(
        _ffn_body,
        grid=(t // tm,),
        in_specs=[pl.BlockSpec((tm, d), lambda i: (i, 0)),
                  _const_spec(gain.shape), _const_spec(wi.shape), _const_spec(wo.shape)],
        out_specs=pl.BlockSpec((tm, d), lambda i: (i, 0)),
        out_shape=jax.ShapeDtypeStruct((t, d), F32),
        compiler_params=_cparams(("parallel",)),
        name="ffn",
    )(x, gain, wi, wo)


def _inproj_body(x_ref, g_ref, w_ref, b_ref, *out_refs):
    u = _rms(x_ref[...], g_ref[...]).astype(BF16)
    off = 0
    for o_ref, width in zip(out_refs, OUT_WIDTHS):
        step = min(width, 1024)
        for c in range(0, width, step):
            z = _dot(u, w_ref[:, off + c:off + c + step]) + b_ref[:, off + c:off + c + step]
            o_ref[:, c:c + step] = z.astype(o_ref.dtype)
        off += width


def _inproj(x, gain, w, b):
    t, d = x.shape
    tm = min(TM_DENSE, t)
    dtypes = (BF16, BF16, BF16, BF16, BF16, F32)
    return pl.pallas_call(
        _inproj_body,
        grid=(t // tm,),
        in_specs=[pl.BlockSpec((tm, d), lambda i: (i, 0)),
                  _const_spec(gain.shape), _const_spec(w.shape), _const_spec(b.shape)],
        out_specs=[pl.BlockSpec((tm, wd), lambda i: (i, 0)) for wd in OUT_WIDTHS],
        out_shape=[jax.ShapeDtypeStruct((t, wd), dt) for wd, dt in zip(OUT_WIDTHS, dtypes)],
        compiler_params=_cparams(("parallel",)),
        name="inproj",
    )(x, gain, w, b)


def _merge_body(x_ref, gt_ref, ysb_ref, yml_ref, ymla_ref, wsb_ref, wml_ref, wmla_ref, wout_ref,
                g2_ref, wi_ref, wo_ref, o_ref):
    d = D_MODEL
    merged = jax.nn.sigmoid(gt_ref[:, 0:d].astype(F32)) * _dot(ysb_ref[...], wsb_ref[...])
    merged += jax.nn.sigmoid(gt_ref[:, d:2 * d].astype(F32)) * _dot(yml_ref[...], wml_ref[...])
    merged += jax.nn.sigmoid(gt_ref[:, 2 * d:3 * d].astype(F32)) * _dot(ymla_ref[...], wmla_ref[...])
    x = x_ref[...] + _dot(merged.astype(BF16), wout_ref[...])
    o_ref[...] = _swiglu_residual(x, g2_ref[...], wi_ref, wo_ref)


def _merge_ffn(x, gates, ysb, yml, ymla, wsb, wml, wmla, wout, g2, wi, wo):
    t, d = x.shape
    tm = min(TM_DENSE, t)
    row = lambda a: pl.BlockSpec((tm, a.shape[1]), lambda i: (i, 0))
    consts = (wsb, wml, wmla, wout, g2, wi, wo)
    return pl.pallas_call(
        _merge_body,
        grid=(t // tm,),
        in_specs=[row(x), row(gates), row(ysb), row(yml), row(ymla)] + [_const_spec(c.shape) for c in consts],
        out_specs=pl.BlockSpec((tm, d), lambda i: (i, 0)),
        out_shape=jax.ShapeDtypeStruct((t, d), F32),
        compiler_params=_cparams(("parallel",)),
        name="merge_ffn",
    )(x, gates, ysb, yml, ymla, *consts)


def _sb_body(q_ref, k_ref, v_ref, o_ref, qm_ref, *, tq):
    qi = pl.program_id(1)
    lane = lax.broadcasted_iota(jnp.int32, (1, LANES), 1)
    qpos = lax.broadcasted_iota(jnp.int32, (tq, tq), 0)
    kpos = lax.broadcasted_iota(jnp.int32, (tq, tq), 1)
    tri = jnp.where(qpos > kpos, 1.0, 0.0).astype(BF16)
    strict = kpos < qpos

    scale = SB_HEAD_DIM ** -0.5
    for h in range(SB_HEADS):
        qp = q_ref[0, :, (h // 2) * LANES:(h // 2 + 1) * LANES]
        lo = (h % 2) * SB_HEAD_DIM
        qm_ref[h] = jnp.where((lane >= lo) & (lane < lo + SB_HEAD_DIM), qp, 0) * scale

    def sweep(kb, carry, masked):
        start = pl.multiple_of(kb * tq, tq)
        new = []
        for h in range(SB_HEADS):
            pair = slice((h // 2) * LANES, (h // 2 + 1) * LANES)
            run, acc = carry[h]
            s = _dot_nt(qm_ref[h], k_ref[0, pl.ds(start, tq), pair])
            ls = jnp.minimum(s, 0.0) - jnp.log(1.0 + jnp.exp(-jnp.abs(s)))
            lk = ls - s
            if masked:
                lk = jnp.where(strict, lk, 0.0)
            w = jnp.exp(ls + _dot(lk.astype(BF16), tri) + run)
            if masked:
                w = jnp.where(strict, w, 0.0)
            acc = acc + _dot(w.astype(BF16), v_ref[0, pl.ds(start, tq), pair])
            run = run + jnp.sum(lk, axis=-1, keepdims=True)
            new.append((run, acc))
        return tuple(new)

    def alive(carry):
        top = functools.reduce(jnp.maximum, [run for run, _ in carry])
        return (jnp.max(top) >= SB_EXP_UNDERFLOW).astype(jnp.int32)

    def body(state):
        kb, _, carry = state
        carry = sweep(kb, carry, False)
        return kb - 1, alive(carry), carry

    init = tuple((jnp.zeros((tq, 1), F32), jnp.zeros((tq, LANES), F32)) for _ in range(SB_HEADS))
    carry = sweep(qi, init, True)
    _, _, carry = lax.while_loop(lambda st: (st[0] >= 0) & (st[1] > 0), body, (qi - 1, alive(carry), carry))
    for p in range(SB_HEADS // 2):
        o_ref[0, :, p * LANES:(p + 1) * LANES] = jnp.where(
            lane < SB_HEAD_DIM, carry[2 * p][1], carry[2 * p + 1][1]).astype(o_ref.dtype)


def _sb_attention(sbqkv):
    b, s, _ = sbqkv.shape
    tq = min(TQ_ATTN, s)
    return pl.pallas_call(
        functools.partial(_sb_body, tq=tq),
        grid=(b, s // tq),
        in_specs=[pl.BlockSpec((1, tq, SB_WIDTH), lambda bi, qi: (bi, qi, 0)),
                  pl.BlockSpec((1, s, SB_WIDTH), lambda bi, qi: (bi, 0, 1)),
                  pl.BlockSpec((1, s, SB_WIDTH), lambda bi, qi: (bi, 0, 2))],
        out_specs=pl.BlockSpec((1, tq, SB_WIDTH), lambda bi, qi: (bi, qi, 0)),
        out_shape=jax.ShapeDtypeStruct((b, s, SB_WIDTH), BF16),
        scratch_shapes=[pltpu.VMEM((SB_HEADS, tq, LANES), BF16)],
        compiler_params=_cparams(("parallel", "arbitrary")),
        name="sb_attention",
    )(sbqkv, sbqkv, sbqkv)


def _split3(x):
    p1 = x.astype(BF16)
    r1 = x - p1.astype(F32)
    p2 = r1.astype(BF16)
    p3 = (r1 - p2.astype(F32)).astype(BF16)
    return p1, p2, p3


def _mlstm_body(qk_ref, vo_ref, if_ref, cw_ref, cb_ref, og_ref, o_ref, ext_ref, c_ref, n_ref, m_ref):
    L = ML_CHUNK
    hd = ML_HEAD_DIM
    tail = 8

    @pl.when(pl.program_id(1) == 0)
    def _():
        ext_ref[0:tail, :] = jnp.zeros((tail, 2 * ML_WIDTH), F32)
        c_ref[...] = jnp.zeros(c_ref.shape, F32)
        n_ref[...] = jnp.zeros(n_ref.shape, F32)
        m_ref[...] = jnp.zeros(m_ref.shape, F32)

    raw = qk_ref[0].astype(F32)
    ext_ref[tail:tail + L, :] = raw
    conv = cb_ref[...] + cw_ref[ML_CONV - 1:ML_CONV, :] * raw
    for j in range(1, ML_CONV):
        conv += cw_ref[ML_CONV - 1 - j:ML_CONV - j, :] * ext_ref[tail - j:tail - j + L, :]
    ext_ref[0:tail, :] = ext_ref[L:L + tail, :]
    qk = conv * jax.nn.sigmoid(conv)

    gl = lax.broadcasted_iota(jnp.int32, (1, LANES), 1)
    gin = if_ref[0]
    gates = jnp.where(gl < ML_HEADS, gin, jax.nn.log_sigmoid(gin))
    rr = lax.broadcasted_iota(jnp.int32, (L, L), 0)
    cc = lax.broadcasted_iota(jnp.int32, (L, L), 1)
    causal = rr >= cc
    tril = jnp.where(causal, 1.0, 0.0).astype(BF16)
    cum = sum(_dot(tril, p) for p in _split3(gates))
    gates_t = gates.T
    cum_t = cum.T

    for h in range(ML_HEADS):
        qf = qk[:, h * hd:(h + 1) * hd]
        qh = qf.astype(BF16)
        kf =qk[:, ML_WIDTH + h * hd:ML_WIDTH + (h + 1) * hd] * (hd ** -0.5)
        kh = kf.astype(BF16)
        vh = vo_ref[0, :, h * hd:(h + 1) * hd]
        c_prev = c_ref[h]
        n_prev = n_ref[h:h + 1, :]
        m_prev = m_ref[h:h + 1, 0:1]

        cf_col = cum[:, ML_HEADS + h:ML_HEADS + h + 1]
        ig_col = gates[:, h:h + 1]
        cf_row = cum_t[ML_HEADS + h:ML_HEADS + h + 1, :]
        ig_row = gates_t[h:h + 1, :]

        log_intra = jnp.where(causal, cf_col - cf_row + ig_row, -jnp.inf)
        log_inter = cf_col + m_prev
        m = jnp.maximum(log_inter, jnp.max(log_intra, axis=-1, keepdims=True))
        w_intra = jnp.exp(log_intra - m)
        w_inter = jnp.exp(log_inter - m)
        scores = _dot_nt(qh, kh) * w_intra
        num = _dot(scores.astype(BF16), vh) + w_inter * _dot(qh, c_prev.astype(BF16))
        qn = jnp.sum(qf * n_prev, axis=-1, keepdims=True)
        den = jnp.sum(scores, axis=-1, keepdims=True) + w_inter * qn
        hout = num / jnp.maximum(jnp.abs(den), jnp.exp(-m))

        f_total = cf_col[L - 1:L, :]
        log_to_end = f_total - cf_col + ig_col
        m_new = jnp.maximum(f_total + m_prev, jnp.max(log_to_end, axis=0, keepdims=True))
        decay = jnp.exp(f_total + m_prev - m_new)
        w_end = jnp.exp(log_to_end - m_new)
        kw = kf * w_end
        c_ref[h] = decay * c_prev + _dot_tn(kw.astype(BF16), vh)
        n_ref[h:h + 1, :] = decay * n_prev + jnp.sum(kw, axis=0, keepdims=True)
        m_ref[h:h + 1, :] = jnp.broadcast_to(m_new, (1, LANES))

        hn = _rms(hout, og_ref[:, h * hd:(h + 1) * hd])
        o_pre = vo_ref[0, :, ML_WIDTH + h * hd:ML_WIDTH + (h + 1) * hd].astype(F32)
        o_ref[0, :, h * hd:(h + 1) * hd] = (jax.nn.sigmoid(o_pre) * hn).astype(o_ref.dtype)


def _mlstm(mlqk, mlvo, gif, conv_w, conv_b, out_gain):
    b, s, _ = mlqk.shape
    L = ML_CHUNK
    blk = lambda w: pl.BlockSpec((1, L, w), lambda bi, ci: (bi, ci, 0))
    return pl.pallas_call(
        _mlstm_body,
        grid=(b, s // L),
        in_specs=[blk(W_MLQK), blk(W_MLVO), blk(W_IF),
                  _const_spec(conv_w.shape), _const_spec(conv_b.shape), _const_spec(out_gain.shape)],
        out_specs=blk(ML_WIDTH),
        out_shape=jax.ShapeDtypeStruct((b, s, ML_WIDTH), BF16),
        scratch_shapes=[pltpu.VMEM((L + 8, W_MLQK), F32),
                        pltpu.VMEM((ML_HEADS, ML_HEAD_DIM, ML_HEAD_DIM), F32),
                        pltpu.VMEM((8, ML_HEAD_DIM), F32),
                        pltpu.VMEM((8, LANES), F32)],
        compiler_params=_cparams(("parallel", "arbitrary")),
        name="mlstm",
    )(mlqk, mlvo, gif, conv_w, conv_b, out_gain)


def _rope_table_body(pos_ref, invf_ref, sign_ref, cos_ref, sin_ref):
    ang = pos_ref[...].astype(F32) * invf_ref[...]
    cos_ref[...] = jnp.cos(ang)
    sin_ref[...] = jnp.sin(ang) * sign_ref[...]


def _rope_tables(positions):
    t = positions.size
    half = MLA_ROPE // 2
    inv_freq = jnp.power(ROPE_THETA, -jnp.arange(half, dtype=F32) / half)
    zeros = jnp.zeros((half,), F32)
    pad = jnp.zeros((LANES - MLA_QK,), F32)
    nope = jnp.zeros((MLA_NOPE,), F32)
    invf = jnp.concatenate([nope, inv_freq, inv_freq, pad])[None, :]
    sign = jnp.concatenate([nope, zeros - 1.0, zeros + 1.0, pad])[None, :]
    tm = min(TM_DENSE, t)
    return pl.pallas_call(
        _rope_table_body,
        grid=(t // tm,),
        in_specs=[pl.BlockSpec((tm, 1), lambda i: (i, 0)), _const_spec(invf.shape), _const_spec(sign.shape)],
        out_specs=[pl.BlockSpec((tm, LANES), lambda i: (i, 0))] * 2,
        out_shape=[jax.ShapeDtypeStruct((t, LANES), F32)] * 2,
        compiler_params=_cparams(("parallel",)),
        name="rope_tables",
    )(positions.reshape(t, 1), invf, sign)


def _mla_prep_body(c_ref, cos_ref, sin_ref, qn_ref, kvn_ref, wq_ref, wk_ref, wv_ref, qg_ref, kg_ref,
                   q_out, k_out, v_out):
    c = c_ref[...].astype(F32)
    cq = _rms(c[:, :MLA_Q_RANK], qn_ref[...]).astype(BF16)
    ckv = _rms(c[:, MLA_Q_RANK:MLA_Q_RANK + MLA_KV_RANK], kvn_ref[...]).astype(BF16)
    k_rope = c[:, MLA_Q_RANK + MLA_KV_RANK:]
    q = _dot(cq, wq_ref[...])
    k = _dot(ckv, wk_ref[...])
    v_out[...] = _dot(ckv, wv_ref[...]).astype(v_out.dtype)
    cos = cos_ref[...]
    sin = sin_ref[...]
    lane = lax.broadcasted_iota(jnp.int32, (1, LANES), 1)
    first_half = lane < MLA_NOPE + MLA_ROPE // 2

    def norm_rope(x, gain):
        y = x * lax.rsqrt(jnp.sum(x * x, axis=-1, keepdims=True) * (1.0 / MLA_QK) + EPS) * gain
        rot = jnp.where(first_half, pltpu.roll(y, LANES - MLA_ROPE // 2, 1), pltpu.roll(y, MLA_ROPE // 2, 1))
        return y * cos + rot * sin

    for h in range(MLA_HEADS):
        sl = slice(h * LANES, (h + 1) * LANES)
        q_out[:, sl] = norm_rope(q[:, sl], qg_ref[...]).astype(q_out.dtype)
        k_out[:, sl] = norm_rope(k[:, sl] + k_rope, kg_ref[...]).astype(k_out.dtype)


def _mla_prep(mla_c, cos, sin, qn, kvn, wq, wk, wv, qg, kg):
    t = mla_c.shape[0]
    tm = min(TM_DENSE, t)
    row = lambda w: pl.BlockSpec((tm, w), lambda i: (i, 0))
    consts = (qn, kvn, wq, wk, wv, qg, kg)
    widths = (MLA_HEADS * LANES, MLA_HEADS * LANES, MLA_WIDTH)
    return pl.pallas_call(
        _mla_prep_body,
        grid=(t // tm,),
        in_specs=[row(W_MLA), row(LANES), row(LANES)] + [_const_spec(c.shape) for c in consts],
        out_specs=[row(w) for w in widths],
        out_shape=[jax.ShapeDtypeStruct((t, w), BF16) for w in widths],
        compiler_params=_cparams(("parallel",)),
        name="mla_prep",
    )(mla_c, cos, sin, *consts)


def _mla_attn_body(q_ref, k_ref, v_ref, o_ref, *, tq):
    qi = pl.program_id(1)
    lane = lax.broadcasted_iota(jnp.int32, (1, LANES), 1)
    qpos = lax.broadcasted_iota(jnp.int32, (tq, tq), 0)
    kpos = lax.broadcasted_iota(jnp.int32, (tq, tq), 1)
    causal = kpos <= qpos
    ones = jnp.ones((tq, LANES), BF16)

    def step(kb, carry, masked):
        start = pl.multiple_of(kb * tq, tq)
        new = []
        for h in range(MLA_HEADS):
            m, acc = carry[h]
            pair = slice((h // 2) * LANES, (h // 2 + 1) * LANES)
            q = q_ref[0, :, h * LANES:(h + 1) * LANES]
            k = k_ref[0, pl.ds(start, tq), h * LANES:(h + 1) * LANES]
            v_ext = jnp.concatenate([v_ref[0, pl.ds(start, tq), pair], ones], axis=1)
            s = _dot_nt(q, k)
            if masked:
                s = jnp.where(causal, s, -jnp.inf)
            m_new = jnp.maximum(m, jnp.max(s, axis=-1, keepdims=True))
            acc = jnp.exp(m - m_new) * acc + _dot(jnp.exp(s - m_new).astype(BF16), v_ext)
            new.append((m_new, acc))
        return tuple(new)

    init = tuple((jnp.full((tq, 1), -jnp.inf, F32), jnp.zeros((tq, 2 * LANES), F32)) for _ in range(MLA_HEADS))
    carry = lax.fori_loop(0, qi, functools.partial(step, masked=False), init)
    carry = step(qi, carry, True)
    outs = [acc[:, :LANES] / acc[:, LANES:] for _, acc in carry]
    for p in range(MLA_HEADS // 2):
        o_ref[0, :, p * LANES:(p + 1) * LANES] = jnp.where(
            lane < MLA_V, outs[2 * p], outs[2 * p + 1]).astype(o_ref.dtype)


def _mla_attention(q, k, v):
    b, s, _ = q.shape
    tq = min(TQ_ATTN, s)
    tile = lambda w: pl.BlockSpec((1, tq, w), lambda bi, qi: (bi, qi, 0))
    full = lambda w: pl.BlockSpec((1, s, w), lambda bi, qi: (bi, 0, 0))
    return pl.pallas_call(
        functools.partial(_mla_attn_body, tq=tq),
        grid=(b, s // tq),
        in_specs=[tile(MLA_HEADS * LANES), full(MLA_HEADS * LANES), full(MLA_WIDTH)],
        out_specs=tile(MLA_WIDTH),
        out_shape=jax.ShapeDtypeStruct((b, s, MLA_WIDTH), BF16),
        compiler_params=_cparams(("parallel", "arbitrary")),
        name="mla_attention",
    )(q, k, v)


def _seg(z, name):
    off = 0
    for n, w in SEGMENTS:
        if n == name:
            return z[..., off:off + w]
        off += w
    raise KeyError(name)


def _pack_inproj(w, b):
    wb = jnp.concatenate([w, b[None, :]], axis=0)
    z = lambda n: jnp.zeros((wb.shape[0], n), wb.dtype)
    kr = jnp.concatenate([z(MLA_NOPE), _seg(wb, 'mla_kr'), z(LANES - MLA_QK)], axis=1)
    gif = jnp.concatenate([_seg(wb, 'ml_i'), _seg(wb, 'ml_f'), z(LANES - 2 * ML_HEADS)], axis=1)
    packed = jnp.concatenate(
        [_seg(wb, 'gates'), _seg(wb, 'ml_q'), _seg(wb, 'ml_k'), _seg(wb, 'ml_v'), _seg(wb, 'ml_o'),
         _seg(wb, 'sb_q'), _seg(wb, 'sb_k'), _seg(wb, 'sb_v'), _seg(wb, 'mla_cq'), _seg(wb, 'mla_ckv'),
         kr, gif], axis=1)
    return packed[:-1].astype(BF16), packed[-1:]


def _pad_heads(w, width, keep):
    r = w.shape[0]
    wh = w.reshape(r, MLA_HEADS, width)[:, :, :keep]
    return jnp.pad(wh, ((0, 0), (0, 0), (0, LANES - keep))).reshape(r, MLA_HEADS * LANES)


def kernel(x, positions, ffn1_norm, ffn1_wi, ffn1_wo, mix_norm, w_in, b_in, ml_conv_w, ml_conv_b,
           ml_out_norm, mla_q_norm, mla_kv_norm, mla_wq_up, mla_wkv_up, mla_q_gain, mla_k_gain,
           w_up_sb, w_up_ml, w_up_mla, w_out, ffn2_norm, ffn2_wi, ffn2_wo):
    b, s, d = x.shape
    t = b * s
    depth = w_in.shape[0]
    cos, sin = _rope_tables(positions)
    xt = x.reshape(t, d)
    row = lambda a: a[None, :]
    pad_gain = lambda g: jnp.pad(g, (0, LANES - MLA_QK))[None, :]
    for l in range(depth):
        xt = _ffn(xt, row(ffn1_norm[l]), ffn1_wi[l].astype(BF16), ffn1_wo[l].astype(BF16))

        w_packed, b_packed = _pack_inproj(w_in[l], b_in[l])
        gates, mlqk, mlvo, sbqkv, mla_c, gif = _inproj(xt, row(mix_norm[l]), w_packed, b_packed)

        ysb = _sb_attention(sbqkv.reshape(b, s, W_SB))

        yml = _mlstm(mlqk.reshape(b, s, W_MLQK), mlvo.reshape(b, s, W_MLVO), gif.reshape(b, s, W_IF),
                     ml_conv_w[l], row(ml_conv_b[l]), row(ml_out_norm[l]))

        wkv = mla_wkv_up[l]
        wv = wkv.reshape(MLA_KV_RANK, MLA_HEADS, MLA_NOPE + MLA_V)[:, :, MLA_NOPE:].reshape(MLA_KV_RANK, MLA_WIDTH)
        qh, kh, vh = _mla_prep(
            mla_c, cos, sin, row(mla_q_norm[l]), row(mla_kv_norm[l]),
            _pad_heads(mla_wq_up[l], MLA_QK, MLA_QK).astype(BF16),
            _pad_heads(wkv, MLA_NOPE + MLA_V, MLA_NOPE).astype(BF16), wv.astype(BF16),
            pad_gain(mla_q_gain[l] * MLA_QK ** -0.5), pad_gain(mla_k_gain[l]))
        ymla = _mla_attention(qh.reshape(b, s, -1), kh.reshape(b, s, -1), vh.reshape(b, s, -1))

        xt = _merge_ffn(xt, gates, ysb.reshape(t, -1), yml.reshape(t, -1), ymla.reshape(t, -1),
                        w_up_sb[l].astype(BF16), w_up_ml[l].astype(BF16), w_up_mla[l].astype(BF16),
                        w_out[l].astype(BF16), row(ffn2_norm[l]), ffn2_wi[l].astype(BF16), ffn2_wo[l].astype(BF16))
    return xt.reshape(b, s, d)
```

```python
import functools

import numpy as np
import jax
import jax.numpy as jnp
from jax import lax
from jax.experimental import pallas as pl
from jax.experimental.pallas import tpu as pltpu

F32 = jnp.float32
BF16 = jnp.bfloat16

EPS = 1e-6
LANES = 128
V7X_VMEM_LIMIT = 56 * 1024 * 1024

D_MODEL = 1024
FFN_DIM = 1408
SB_HEADS, SB_HEAD_DIM = 4, 64
SB_WIDTH = SB_HEADS * SB_HEAD_DIM
ML_HEADS, ML_HEAD_DIM = 4, 128
ML_WIDTH = ML_HEADS * ML_HEAD_DIM
ML_CHUNK = 128
ML_CONV = 4
MLA_HEADS, MLA_NOPE, MLA_ROPE, MLA_V = 4, 64, 32, 64
MLA_QK = MLA_NOPE + MLA_ROPE
MLA_WIDTH = MLA_HEADS * MLA_V
MLA_Q_RANK, MLA_KV_RANK = 256, 128
ROPE_THETA = 10000.0
N_BRANCH = 3

SEGMENTS = (
    ('sb_q', SB_WIDTH), ('sb_k', SB_WIDTH), ('sb_v', SB_WIDTH),
    ('ml_q', ML_WIDTH), ('ml_k', ML_WIDTH), ('ml_v', ML_WIDTH), ('ml_o', ML_WIDTH),
    ('ml_i', ML_HEADS), ('ml_f', ML_HEADS),
    ('mla_cq', MLA_Q_RANK), ('mla_ckv', MLA_KV_RANK), ('mla_kr', MLA_ROPE),
    ('gates', N_BRANCH * D_MODEL),
)

W_GATES = N_BRANCH * D_MODEL
W_MLQK = 2 * ML_WIDTH
W_MLVO = 2 * ML_WIDTH
W_SB = 3 * SB_WIDTH
W_MLA = MLA_Q_RANK + MLA_KV_RANK + LANES
W_IF = LANES
OUT_WIDTHS = (W_GATES, W_MLQK, W_MLVO, W_SB, W_MLA, W_IF)
N_PACKED = sum(OUT_WIDTHS)

TM_DENSE = 512
TQ_ATTN = 256
MLA_TQ = 512
MLA_TK = 256
MLA_ROWS = 32
SB_EXP_UNDERFLOW = -120.0


def _cparams(sem):
    return pltpu.CompilerParams(dimension_semantics=sem, vmem_limit_bytes=V7X_VMEM_LIMIT)


def _const_spec(shape):
    nd = len(shape)
    return pl.BlockSpec(shape, lambda *_: (0,) * nd)


def _rms(x, gain):
    return x * lax.rsqrt(jnp.mean(x * x, axis=-1, keepdims=True) + EPS) * gain


def _dot(a, b):
    return jnp.dot(a, b, preferred_element_type=F32)


def _dot_nt(a, b):
    return lax.dot_general(a, b, (((1,), (1,)), ((), ())), preferred_element_type=F32)


def _dot_tn(a, b):
    return lax.dot_general(a, b, (((0,), (0,)), ((), ())), preferred_element_type=F32)


def _swiglu_residual(x, gain, wi_ref, wo_ref):
    u = _rms(x, gain).astype(BF16)
    h = _dot(u, wi_ref[...])
    a = h[:, :FFN_DIM]
    g = h[:, FFN_DIM:]
    act = (a * jax.nn.sigmoid(a) * g).astype(BF16)
    return x + 0.5 * _dot(act, wo_ref[...])


def _ffn_body(x_ref, g_ref, wi_ref, wo_ref, o_ref):
    o_ref[...] = _swiglu_residual(x_ref[...], g_ref[...], wi_ref, wo_ref)


def _ffn(x, gain, wi, wo):
    t, d = x.shape
    tm = min(TM_DENSE, t)
    return pl.pallas_call(
        _ffn_body,
        grid=(t // tm,),
        in_specs=[pl.BlockSpec((tm, d), lambda i: (i, 0)),
                  _const_spec(gain.shape), _const_spec(wi.shape), _const_spec(wo.shape)],
        out_specs=pl.BlockSpec((tm, d), lambda i: (i, 0)),
        out_shape=jax.ShapeDtypeStruct((t, d), F32),
        compiler_params=_cparams(("parallel",)),
        name="ffn",
    )(x, gain, wi, wo)


def _inproj_body(x_ref, g_ref, w_ref, b_ref, *out_refs):
    u = _rms(x_ref[...], g_ref[...]).astype(BF16)
    off = 0
    for o_ref, width in zip(out_refs, OUT_WIDTHS):
        step = min(width, 1024)
        for c in range(0, width, step):
            z = _dot(u, w_ref[:, off + c:off + c + step]) + b_ref[:, off + c:off + c + step]
            o_ref[:, c:c + step] = z.astype(o_ref.dtype)
        off += width


def _inproj(x, gain, w, b):
    t, d = x.shape
    tm = min(TM_DENSE, t)
    dtypes = (BF16, BF16, BF16, BF16, BF16, F32)
    return pl.pallas_call(
        _inproj_body,
        grid=(t // tm,),
        in_specs=[pl.BlockSpec((tm, d), lambda i: (i, 0)),
                  _const_spec(gain.shape), _const_spec(w.shape), _const_spec(b.shape)],
        out_specs=[pl.BlockSpec((tm, wd), lambda i: (i, 0)) for wd in OUT_WIDTHS],
        out_shape=[jax.ShapeDtypeStruct((t, wd), dt) for wd, dt in zip(OUT_WIDTHS, dtypes)],
        compiler_params=_cparams(("parallel",)),
        name="inproj",
    )(x, gain, w, b)


def _merge_body(x_ref, gt_ref, ysb_ref, yml_ref, ymla_ref, wsb_ref, wml_ref, wmla_ref, wout_ref,
                g2_ref, wi_ref, wo_ref, o_ref):
    d = D_MODEL
    merged = jax.nn.sigmoid(gt_ref[:, 0:d].astype(F32)) * _dot(ysb_ref[...], wsb_ref[...])
    merged += jax.nn.sigmoid(gt_ref[:, d:2 * d].astype(F32)) * _dot(yml_ref[...], wml_ref[...])
    merged += jax.nn.sigmoid(gt_ref[:, 2 * d:3 * d].astype(F32)) * _dot(ymla_ref[...], wmla_ref[...])
    x = x_ref[...] + _dot(merged.astype(BF16), wout_ref[...])
    o_ref[...] = _swiglu_residual(x, g2_ref[...], wi_ref, wo_ref)


def _merge_ffn(x, gates, ysb, yml, ymla, wsb, wml, wmla, wout, g2, wi, wo):
    t, d = x.shape
    tm = min(TM_DENSE, t)
    row = lambda a: pl.BlockSpec((tm, a.shape[1]), lambda i: (i, 0))
    consts = (wsb, wml, wmla, wout, g2, wi, wo)
    return pl.pallas_call(
        _merge_body,
        grid=(t // tm,),
        in_specs=[row(x), row(gates), row(ysb), row(yml), row(ymla)] + [_const_spec(c.shape) for c in consts],
        out_specs=pl.BlockSpec((tm, d), lambda i: (i, 0)),
        out_shape=jax.ShapeDtypeStruct((t, d), F32),
        compiler_params=_cparams(("parallel",)),
        name="merge_ffn",
    )(x, gates, ysb, yml, ymla, *consts)


def _sb_body(q_ref, k_ref, v_ref, o_ref, qm_ref, *, tq):
    qi = pl.program_id(1)
    lane = lax.broadcasted_iota(jnp.int32, (1, LANES), 1)
    qpos = lax.broadcasted_iota(jnp.int32, (tq, tq), 0)
    kpos = lax.broadcasted_iota(jnp.int32, (tq, tq), 1)
    tri = jnp.where(qpos > kpos, 1.0, 0.0).astype(BF16)
    strict = kpos < qpos

    scale = SB_HEAD_DIM ** -0.5
    for h in range(SB_HEADS):
        qp = q_ref[0, :, (h // 2) * LANES:(h // 2 + 1) * LANES]
        lo = (h % 2) * SB_HEAD_DIM
        qm_ref[h] = jnp.where((lane >= lo) & (lane < lo + SB_HEAD_DIM), qp, 0) * scale

    def sweep(kb, carry, masked):
        start = pl.multiple_of(kb * tq, tq)
        new = []
        for h in range(SB_HEADS):
            pair = slice((h // 2) * LANES, (h // 2 + 1) * LANES)
            run, acc = carry[h]
            s = _dot_nt(qm_ref[h], k_ref[0, pl.ds(start, tq), pair])
            ls = jnp.minimum(s, 0.0) - jnp.log(1.0 + jnp.exp(-jnp.abs(s)))
            lk = ls - s
            if masked:
                lk = jnp.where(strict, lk, 0.0)
            w = jnp.exp(ls + _dot(lk.astype(BF16), tri) + run)
            if masked:
                w = jnp.where(strict, w, 0.0)
            acc = acc + _dot(w.astype(BF16), v_ref[0, pl.ds(start, tq), pair])
            run = run + jnp.sum(lk, axis=-1, keepdims=True)
            new.append((run, acc))
        return tuple(new)

    def alive(carry):
        top = functools.reduce(jnp.maximum, [run for run, _ in carry])
        return (jnp.max(top) >= SB_EXP_UNDERFLOW).astype(jnp.int32)

    def body(state):
        kb, _, carry = state
        carry = sweep(kb, carry, False)
        return kb - 1, alive(carry), carry

    init = tuple((jnp.zeros((tq, 1), F32), jnp.zeros((tq, LANES), F32)) for _ in range(SB_HEADS))
    carry = sweep(qi, init, True)
    _, _, carry = lax.while_loop(lambda st: (st[0] >= 0) & (st[1] > 0), body, (qi - 1, alive(carry), carry))
    for p in range(SB_HEADS // 2):
        o_ref[0, :, p * LANES:(p + 1) * LANES] = jnp.where(
            lane < SB_HEAD_DIM, carry[2 * p][1], carry[2 * p + 1][1]).astype(o_ref.dtype)


def _sb_attention(sbqkv):
    b, s, _ = sbqkv.shape
    tq = min(TQ_ATTN, s)
    return pl.pallas_call(
        functools.partial(_sb_body, tq=tq),
        grid=(b, s // tq),
        in_specs=[pl.BlockSpec((1, tq, SB_WIDTH), lambda bi, qi: (bi, qi, 0)),
                  pl.BlockSpec((1, s, SB_WIDTH), lambda bi, qi: (bi, 0, 1)),
                  pl.BlockSpec((1, s, SB_WIDTH), lambda bi, qi: (bi, 0, 2))],
        out_specs=pl.BlockSpec((1, tq, SB_WIDTH), lambda bi, qi: (bi, qi, 0)),
        out_shape=jax.ShapeDtypeStruct((b, s, SB_WIDTH), BF16),
        scratch_shapes=[pltpu.VMEM((SB_HEADS, tq, LANES), BF16)],
        compiler_params=_cparams(("parallel", "arbitrary")),
        name="sb_attention",
    )(sbqkv, sbqkv, sbqkv)


def _split3(x):
    p1 = x.astype(BF16)
    r1 = x - p1.astype(F32)
    p2 = r1.astype(BF16)
    p3 = (r1 - p2.astype(F32)).astype(BF16)
    return p1, p2, p3


def _mlstm_body(qk_ref, vo_ref, if_ref, cw_ref, cb_ref, og_ref, o_ref, ext_ref, c_ref, n_ref, m_ref):
    L = ML_CHUNK
    hd = ML_HEAD_DIM
    tail = 8

    @pl.when(pl.program_id(1) == 0)
    def _():
        ext_ref[0:tail, :] = jnp.zeros((tail, 2 * ML_WIDTH), F32)
        c_ref[...] = jnp.zeros(c_ref.shape, F32)
        n_ref[...] = jnp.zeros(n_ref.shape, F32)
        m_ref[...] = jnp.zeros(m_ref.shape, F32)

    raw = qk_ref[0].astype(F32)
    ext_ref[tail:tail + L, :] = raw
    conv = cb_ref[...] + cw_ref[ML_CONV - 1:ML_CONV, :] * raw
    for j in range(1, ML_CONV):
        conv += cw_ref[ML_CONV - 1 - j:ML_CONV - j, :] * ext_ref[tail - j:tail - j + L, :]
    ext_ref[0:tail, :] = ext_ref[L:L + tail, :]
    qk = conv * jax.nn.sigmoid(conv)

    gl = lax.broadcasted_iota(jnp.int32, (1, LANES), 1)
    gin = if_ref[0]
    gates = jnp.where(gl < ML_HEADS, gin, jax.nn.log_sigmoid(gin))
    rr = lax.broadcasted_iota(jnp.int32, (L, L), 0)
    cc = lax.broadcasted_iota(jnp.int32, (L, L), 1)
    causal = rr >= cc
    tril = jnp.where(causal, 1.0, 0.0).astype(BF16)
    cum = sum(_dot(tril, p) for p in _split3(gates))
    gates_t = gates.T
    cum_t = cum.T

    for h in range(ML_HEADS):
        qf = qk[:, h * hd:(h + 1) * hd]
        qh = qf.astype(BF16)
        kf =qk[:, ML_WIDTH + h * hd:ML_WIDTH + (h + 1) * hd] * (hd ** -0.5)
        kh = kf.astype(BF16)
        vh = vo_ref[0, :, h * hd:(h + 1) * hd]
        c_prev = c_ref[h]
        n_prev = n_ref[h:h + 1, :]
        m_prev = m_ref[h:h + 1, 0:1]

        cf_col = cum[:, ML_HEADS + h:ML_HEADS + h + 1]
        ig_col = gates[:, h:h + 1]
        cf_row = cum_t[ML_HEADS + h:ML_HEADS + h + 1, :]
        ig_row = gates_t[h:h + 1, :]

        log_intra = jnp.where(causal, cf_col - cf_row + ig_row, -jnp.inf)
        log_inter = cf_col + m_prev
        m = jnp.maximum(log_inter, jnp.max(log_intra, axis=-1, keepdims=True))
        w_intra = jnp.exp(log_intra - m)
        w_inter = jnp.exp(log_inter - m)
        scores = _dot_nt(qh, kh) * w_intra
        num = _dot(scores.astype(BF16), vh) + w_inter * _dot(qh, c_prev.astype(BF16))
        qn = jnp.sum(qf * n_prev, axis=-1, keepdims=True)
        den = jnp.sum(scores, axis=-1, keepdims=True) + w_inter * qn
        hout = num / jnp.maximum(jnp.abs(den), jnp.exp(-m))

        f_total = cf_col[L - 1:L, :]
        log_to_end = f_total - cf_col + ig_col
        m_new = jnp.maximum(f_total + m_prev, jnp.max(log_to_end, axis=0, keepdims=True))
        decay = jnp.exp(f_total + m_prev - m_new)
        w_end = jnp.exp(log_to_end - m_new)
        kw = kf * w_end
        c_ref[h] = decay * c_prev + _dot_tn(kw.astype(BF16), vh)
        n_ref[h:h + 1, :] = decay * n_prev + jnp.sum(kw, axis=0, keepdims=True)
        m_ref[h:h + 1, :] = jnp.broadcast_to(m_new, (1, LANES))

        hn = _rms(hout, og_ref[:, h * hd:(h + 1) * hd])
        o_pre = vo_ref[0, :, ML_WIDTH + h * hd:ML_WIDTH + (h + 1) * hd].astype(F32)
        o_ref[0, :, h * hd:(h + 1) * hd] = (jax.nn.sigmoid(o_pre) * hn).astype(o_ref.dtype)


def _mlstm(mlqk, mlvo, gif, conv_w, conv_b, out_gain):
    b, s, _ = mlqk.shape
    L = ML_CHUNK
    blk = lambda w: pl.BlockSpec((1, L, w), lambda bi, ci: (bi, ci, 0))
    return pl.pallas_call(
        _mlstm_body,
        grid=(b, s // L),
        in_specs=[blk(W_MLQK), blk(W_MLVO), blk(W_IF),
                  _const_spec(conv_w.shape), _const_spec(conv_b.shape), _const_spec(out_gain.shape)],
        out_specs=blk(ML_WIDTH),
        out_shape=jax.ShapeDtypeStruct((b, s, ML_WIDTH), BF16),
        scratch_shapes=[pltpu.VMEM((L + 8, W_MLQK), F32),
                        pltpu.VMEM((ML_HEADS, ML_HEAD_DIM, ML_HEAD_DIM), F32),
                        pltpu.VMEM((8, ML_HEAD_DIM), F32),
                        pltpu.VMEM((8, LANES), F32)],
        compiler_params=_cparams(("parallel", "arbitrary")),
        name="mlstm",
    )(mlqk, mlvo, gif, conv_w, conv_b, out_gain)


def _rope_table_body(pos_ref, invf_ref, sign_ref, cos_ref, sin_ref):
    ang = pos_ref[...].astype(F32) * invf_ref[...]
    cos_ref[...] = jnp.cos(ang)
    sin_ref[...] = jnp.sin(ang) * sign_ref[...]


def _rope_tables(positions):
    t = positions.size
    half = MLA_ROPE // 2
    inv_freq = jnp.power(ROPE_THETA, -jnp.arange(half, dtype=F32) / half)
    zeros = jnp.zeros((half,), F32)
    pad = jnp.zeros((LANES - MLA_QK,), F32)
    nope = jnp.zeros((MLA_NOPE,), F32)
    invf = jnp.concatenate([nope, inv_freq, inv_freq, pad])[None, :]
    sign = jnp.concatenate([nope, zeros - 1.0, zeros + 1.0, pad])[None, :]
    tm = min(TM_DENSE, t)
    return pl.pallas_call(
        _rope_table_body,
        grid=(t // tm,),
        in_specs=[pl.BlockSpec((tm, 1), lambda i: (i, 0)), _const_spec(invf.shape), _const_spec(sign.shape)],
        out_specs=[pl.BlockSpec((tm, LANES), lambda i: (i, 0))] * 2,
        out_shape=[jax.ShapeDtypeStruct((t, LANES), F32)] * 2,
        compiler_params=_cparams(("parallel",)),
        name="rope_tables",
    )(positions.reshape(t, 1), invf, sign)


def _mla_prep_body(c_ref, cos_ref, sin_ref, qn_ref, kvn_ref, wq_ref, wk_ref, wv_ref, qg_ref, kg_ref,
                   q_out, k_out, v_out):
    c = c_ref[...].astype(F32)
    cq = _rms(c[:, :MLA_Q_RANK], qn_ref[...]).astype(BF16)
    ckv = _rms(c[:, MLA_Q_RANK:MLA_Q_RANK + MLA_KV_RANK], kvn_ref[...]).astype(BF16)
    k_rope = c[:, MLA_Q_RANK + MLA_KV_RANK:]
    q = _dot(cq, wq_ref[...])
    k = _dot(ckv, wk_ref[...])
    vlane = lax.broadcasted_iota(jnp.int32, (1, MLA_HEADS * LANES), 1)
    ones_half = jnp.where(vlane % LANES >= MLA_V, 1.0, 0.0)
    v_out[...] = (_dot(ckv, wv_ref[...]) + ones_half).astype(v_out.dtype)
    cos = cos_ref[...]
    sin = sin_ref[...]
    lane = lax.broadcasted_iota(jnp.int32, (1, LANES), 1)
    first_half = lane < MLA_NOPE + MLA_ROPE // 2

    def norm_rope(x, gain):
        y = x * lax.rsqrt(jnp.sum(x * x, axis=-1, keepdims=True) * (1.0 / MLA_QK) + EPS) * gain
        rot = jnp.where(first_half, pltpu.roll(y, LANES - MLA_ROPE // 2, 1), pltpu.roll(y, MLA_ROPE // 2, 1))
        return y * cos + rot * sin

    for h in range(MLA_HEADS):
        sl = slice(h * LANES, (h + 1) * LANES)
        q_out[:, sl] = norm_rope(q[:, sl], qg_ref[...]).astype(q_out.dtype)
        k_out[:, sl] = norm_rope(k[:, sl] + k_rope, kg_ref[...]).astype(k_out.dtype)


def _mla_prep(mla_c, cos, sin, qn, kvn, wq, wk, wv, qg, kg):
    t = mla_c.shape[0]
    tm = min(TM_DENSE, t)
    row = lambda w: pl.BlockSpec((tm, w), lambda i: (i, 0))
    consts = (qn, kvn, wq, wk, wv, qg, kg)
    widths = (MLA_HEADS * LANES,) * 3
    return pl.pallas_call(
        _mla_prep_body,
        grid=(t // tm,),
        in_specs=[row(W_MLA), row(LANES), row(LANES)] + [_const_spec(c.shape) for c in consts],
        out_specs=[row(w) for w in widths],
        out_shape=[jax.ShapeDtypeStruct((t, w), BF16) for w in widths],
        compiler_params=_cparams(("parallel",)),
        name="mla_prep",
    )(mla_c, cos, sin, *consts)


def _mla_attn_body(q_ref, k_ref, v_ref, o_ref, sa_ref, sb_ref, p_ref, m_ref, al_ref, acc_ref):
    qi = pl.program_id(2)
    tq, tk, rc = MLA_TQ, MLA_TK, MLA_ROWS
    m_ref[...] = jnp.full(m_ref.shape, -jnp.inf, F32)
    acc_ref[...] = jnp.zeros(acc_ref.shape, F32)

    def scores(blk, s_ref, row0=0):
        start = pl.multiple_of(blk * tk, tk)
        for h in range(2):
            hl = slice(h * LANES, (h + 1) * LANES)
            s_ref[h, row0:, :] = _dot_nt(q_ref[0, row0:, hl], k_ref[0, pl.ds(start, tk), hl])

    def softmax(s_ref, diag_block=None, row0=0):
        for h in range(2):
            for r in range(row0, tq, rc):
                s = s_ref[h, r:r + rc, :]
                if diag_block is not None and diag_block * tk + tk - 1 > r:
                    qpos = lax.broadcasted_iota(jnp.int32, (rc, tk), 0) + r
                    kpos = lax.broadcasted_iota(jnp.int32, (rc, tk), 1) + diag_block * tk
                    s = jnp.where(kpos <= qpos, s, -jnp.inf)
                m_old = m_ref[h, r:r + rc, :]
                m_new = jnp.maximum(m_old, jnp.max(s, axis=-1, keepdims=True))
                al_ref[h, r:r + rc, :] = jnp.exp2(m_old - m_new)
                m_ref[h, r:r + rc, :] = m_new
                for c in range(0, tk, LANES):
                    p_ref[h, r:r + rc, c:c + LANES] = jnp.exp2(s[:, c:c + LANES] - m_new).astype(BF16)

    def weighted_values(blk, row0=0):
        start = pl.multiple_of(blk * tk, tk)
        for h in range(2):
            hl = slice(h * LANES, (h + 1) * LANES)
            acc_ref[h, row0:, :] = (al_ref[h, row0:, :] * acc_ref[h, row0:, :]
                                    + _dot(p_ref[h, row0:, :], v_ref[0, pl.ds(start, tk), hl]))

    scores(0, sa_ref)

    def block_pair(j, carry):
        scores(2 * j + 1, sb_ref)
        softmax(sa_ref)
        weighted_values(2 * j)
        scores(2 * j + 2, sa_ref)
        softmax(sb_ref)
        weighted_values(2 * j + 1)
        return carry

    lax.fori_loop(0, qi, block_pair, 0)
    scores(2 * qi + 1, sb_ref, row0=tk)
    softmax(sa_ref, diag_block=0)
    weighted_values(2 * qi)
    softmax(sb_ref, diag_block=1, row0=tk)
    weighted_values(2 * qi + 1, row0=tk)

    lane = lax.broadcasted_iota(jnp.int32, (1, LANES), 1)
    a0, a1 = acc_ref[0], acc_ref[1]
    o0 = a0 * pltpu.roll(1.0 / a0, MLA_V, 1)
    o1 = pltpu.roll(a1, MLA_V, 1) * (1.0 / a1)
    o_ref[0] = jnp.where(lane < MLA_V, o0, o1).astype(o_ref.dtype)


def _mla_attention(q, k, v):
    b, s, _ = q.shape
    tq = MLA_TQ
    assert s % tq == 0
    npair = MLA_HEADS // 2
    return pl.pallas_call(
        _mla_attn_body,
        grid=(b, npair, s // tq),
        in_specs=[pl.BlockSpec((1, tq, 2 * LANES), lambda bi, hp, qi: (bi, qi, hp)),
                  pl.BlockSpec((1, s, 2 * LANES), lambda bi, hp, qi: (bi, 0, hp)),
                  pl.BlockSpec((1, s, 2 * LANES), lambda bi, hp, qi: (bi, 0, hp))],
        out_specs=pl.BlockSpec((1, tq, LANES), lambda bi, hp, qi: (bi, qi, hp)),
        out_shape=jax.ShapeDtypeStruct((b, s, MLA_WIDTH), BF16),
        scratch_shapes=[pltpu.VMEM((2, tq, MLA_TK), F32), pltpu.VMEM((2, tq, MLA_TK), F32),
                        pltpu.VMEM((2, tq, MLA_TK), BF16), pltpu.VMEM((2, tq, LANES), F32),
                        pltpu.VMEM((2, tq, LANES), F32), pltpu.VMEM((2, tq, LANES), F32)],
        compiler_params=_cparams(("parallel", "parallel", "arbitrary")),
        name="mla_attention",
    )(q, k, v)


def _seg(z, name):
    off = 0
    for n, w in SEGMENTS:
        if n == name:
            return z[..., off:off + w]
        off += w
    raise KeyError(name)


def _pack_inproj(w, b):
    wb = jnp.concatenate([w, b[None, :]], axis=0)
    z = lambda n: jnp.zeros((wb.shape[0], n), wb.dtype)
    kr = jnp.concatenate([z(MLA_NOPE), _seg(wb, 'mla_kr'), z(LANES - MLA_QK)], axis=1)
    gif = jnp.concatenate([_seg(wb, 'ml_i'), _seg(wb, 'ml_f'), z(LANES - 2 * ML_HEADS)], axis=1)
    packed = jnp.concatenate(
        [_seg(wb, 'gates'), _seg(wb, 'ml_q'), _seg(wb, 'ml_k'), _seg(wb, 'ml_v'), _seg(wb, 'ml_o'),
         _seg(wb, 'sb_q'), _seg(wb, 'sb_k'), _seg(wb, 'sb_v'), _seg(wb, 'mla_cq'), _seg(wb, 'mla_ckv'),
         kr, gif], axis=1)
    return packed[:-1].astype(BF16), packed[-1:]


def _pad_heads(w, width, lo, hi):
    r = w.shape[0]
    wh = w.reshape(r, MLA_HEADS, width)[:, :, lo:hi]
    return jnp.pad(wh, ((0, 0), (0, 0), (0, LANES - (hi - lo)))).reshape(r, MLA_HEADS * LANES)


def kernel(x, positions, ffn1_norm, ffn1_wi, ffn1_wo, mix_norm, w_in, b_in, ml_conv_w, ml_conv_b,
           ml_out_norm, mla_q_norm, mla_kv_norm, mla_wq_up, mla_wkv_up, mla_q_gain, mla_k_gain,
           w_up_sb, w_up_ml, w_up_mla, w_out, ffn2_norm, ffn2_wi, ffn2_wo):
    b, s, d = x.shape
    t = b * s
    depth = w_in.shape[0]
    cos, sin = _rope_tables(positions)
    xt = x.reshape(t, d)
    row = lambda a: a[None, :]
    pad_gain = lambda g: jnp.pad(g, (0, LANES - MLA_QK))[None, :]
    for l in range(depth):
        xt = _ffn(xt, row(ffn1_norm[l]), ffn1_wi[l].astype(BF16), ffn1_wo[l].astype(BF16))

        w_packed, b_packed = _pack_inproj(w_in[l], b_in[l])
        gates, mlqk, mlvo, sbqkv, mla_c, gif = _inproj(xt, row(mix_norm[l]), w_packed, b_packed)

        ysb = _sb_attention(sbqkv.reshape(b, s, W_SB))

        yml = _mlstm(mlqk.reshape(b, s, W_MLQK), mlvo.reshape(b, s, W_MLVO), gif.reshape(b, s, W_IF),
                     ml_conv_w[l], row(ml_conv_b[l]), row(ml_out_norm[l]))

        wkv = mla_wkv_up[l]
        kvw = MLA_NOPE + MLA_V
        q_scale = MLA_QK ** -0.5 * np.log2(np.e)
        qh, kh, vh = _mla_prep(
            mla_c, cos, sin, row(mla_q_norm[l]), row(mla_kv_norm[l]),
            _pad_heads(mla_wq_up[l], MLA_QK, 0, MLA_QK).astype(BF16),
            _pad_heads(wkv, kvw, 0, MLA_NOPE).astype(BF16), _pad_heads(wkv, kvw, MLA_NOPE, kvw).astype(BF16),
            pad_gain(mla_q_gain[l] * q_scale), pad_gain(mla_k_gain[l]))
        ymla = _mla_attention(qh.reshape(b, s, -1), kh.reshape(b, s, -1), vh.reshape(b, s, -1))

        xt = _merge_ffn(xt, gates, ysb.reshape(t, -1), yml.reshape(t, -1), ymla.reshape(t, -1),
                        w_up_sb[l].astype(BF16), w_up_ml[l].astype(BF16), w_up_mla[l].astype(BF16),
                        w_out[l].astype(BF16), row(ffn2_norm[l]), ffn2_wi[l].astype(BF16), ffn2_wo[l].astype(BF16))
    return xt.reshape(b, s, d)
```

```python
import functools

import numpy as np
import jax
import jax.numpy as jnp
from jax import lax
from jax.experimental import pallas as pl
from jax.experimental.pallas import tpu as pltpu

F32 = jnp.float32
BF16 = jnp.bfloat16

EPS = 1e-6
LANES = 128
V7X_VMEM_LIMIT = 56 * 1024 * 1024

D_MODEL = 1024
FFN_DIM = 1408
SB_HEADS, SB_HEAD_DIM = 4, 64
SB_WIDTH = SB_HEADS * SB_HEAD_DIM
ML_HEADS, ML_HEAD_DIM = 4, 128
ML_WIDTH = ML_HEADS * ML_HEAD_DIM
ML_CHUNK = 128
ML_CONV = 4
MLA_HEADS, MLA_NOPE, MLA_ROPE, MLA_V = 4, 64, 32, 64
MLA_QK = MLA_NOPE + MLA_ROPE
MLA_WIDTH = MLA_HEADS * MLA_V
MLA_Q_RANK, MLA_KV_RANK = 256, 128
ROPE_THETA = 10000.0
N_BRANCH = 3

SEGMENTS = (
    ('sb_q', SB_WIDTH), ('sb_k', SB_WIDTH), ('sb_v', SB_WIDTH),
    ('ml_q', ML_WIDTH), ('ml_k', ML_WIDTH), ('ml_v', ML_WIDTH), ('ml_o', ML_WIDTH),
    ('ml_i', ML_HEADS), ('ml_f', ML_HEADS),
    ('mla_cq', MLA_Q_RANK), ('mla_ckv', MLA_KV_RANK), ('mla_kr', MLA_ROPE),
    ('gates', N_BRANCH * D_MODEL),
)

W_GATES = N_BRANCH * D_MODEL
W_MLQK = 2 * ML_WIDTH
W_MLVO = 2 * ML_WIDTH
W_SB = 3 * SB_WIDTH
W_MLA = MLA_Q_RANK + MLA_KV_RANK + LANES
W_IF = LANES
OUT_WIDTHS = (W_GATES, W_MLQK, W_MLVO, W_SB, W_MLA, W_IF)
N_PACKED = sum(OUT_WIDTHS)

TM_DENSE = 512
TQ_ATTN = 256
ML_TILE = 256
MLA_TQ = 512
MLA_TK = 256
MLA_ROWS = 32
SB_EXP_UNDERFLOW = -120.0


def _cparams(sem):
    return pltpu.CompilerParams(dimension_semantics=sem, vmem_limit_bytes=V7X_VMEM_LIMIT)


def _const_spec(shape):
    nd = len(shape)
    return pl.BlockSpec(shape, lambda *_: (0,) * nd)


def _rms(x, gain):
    return x * lax.rsqrt(jnp.mean(x * x, axis=-1, keepdims=True) + EPS) * gain


def _dot(a, b):
    return jnp.dot(a, b, preferred_element_type=F32)


def _dot_nt(a, b):
    return lax.dot_general(a, b, (((1,), (1,)), ((), ())), preferred_element_type=F32)


def _dot_tn(a, b):
    return lax.dot_general(a, b, (((0,), (0,)), ((), ())), preferred_element_type=F32)


def _swiglu_residual(x, gain, wi_ref, wo_ref):
    u = _rms(x, gain).astype(BF16)
    h = _dot(u, wi_ref[...])
    a = h[:, :FFN_DIM]
    g = h[:, FFN_DIM:]
    act = (a * jax.nn.sigmoid(a) * g).astype(BF16)
    return x + 0.5 * _dot(act, wo_ref[...])


def _ffn_body(x_ref, g_ref, wi_ref, wo_ref, o_ref):
    o_ref[...] = _swiglu_residual(x_ref[...], g_ref[...], wi_ref, wo_ref)


def _ffn(x, gain, wi, wo):
    t, d = x.shape
    tm = min(TM_DENSE, t)
    return pl.pallas_call(
        _ffn_body,
        grid=(t // tm,),
        in_specs=[pl.BlockSpec((tm, d), lambda i: (i, 0)),
                  _const_spec(gain.shape), _const_spec(wi.shape), _const_spec(wo.shape)],
        out_specs=pl.BlockSpec((tm, d), lambda i: (i, 0)),
        out_shape=jax.ShapeDtypeStruct((t, d), F32),
        compiler_params=_cparams(("parallel",)),
        name="ffn",
    )(x, gain, wi, wo)


def _inproj_body(x_ref, g_ref, w_ref, b_ref, *out_refs):
    u = _rms(x_ref[...], g_ref[...]).astype(BF16)
    off = 0
    for o_ref, width in zip(out_refs, OUT_WIDTHS):
        step = min(width, 1024)
        for c in range(0, width, step):
            z = _dot(u, w_ref[:, off + c:off + c + step]) + b_ref[:, off + c:off + c + step]
            o_ref[:, c:c + step] = z.astype(o_ref.dtype)
        off += width


def _inproj(x, gain, w, b):
    t, d = x.shape
    tm = min(TM_DENSE, t)
    dtypes = (BF16, BF16, BF16, BF16, BF16, F32)
    return pl.pallas_call(
        _inproj_body,
        grid=(t // tm,),
        in_specs=[pl.BlockSpec((tm, d), lambda i: (i, 0)),
                  _const_spec(gain.shape), _const_spec(w.shape), _const_spec(b.shape)],
        out_specs=[pl.BlockSpec((tm, wd), lambda i: (i, 0)) for wd in OUT_WIDTHS],
        out_shape=[jax.ShapeDtypeStruct((t, wd), dt) for wd, dt in zip(OUT_WIDTHS, dtypes)],
        compiler_params=_cparams(("parallel",)),
        name="inproj",
    )(x, gain, w, b)


def _merge_body(x_ref, gt_ref, ysb_ref, yml_ref, ymla_ref, wsb_ref, wml_ref, wmla_ref, wout_ref,
                g2_ref, wi_ref, wo_ref, o_ref):
    d = D_MODEL
    merged = jax.nn.sigmoid(gt_ref[:, 0:d].astype(F32)) * _dot(ysb_ref[...], wsb_ref[...])
    merged += jax.nn.sigmoid(gt_ref[:, d:2 * d].astype(F32)) * _dot(yml_ref[...], wml_ref[...])
    merged += jax.nn.sigmoid(gt_ref[:, 2 * d:3 * d].astype(F32)) * _dot(ymla_ref[...], wmla_ref[...])
    x = x_ref[...] + _dot(merged.astype(BF16), wout_ref[...])
    o_ref[...] = _swiglu_residual(x, g2_ref[...], wi_ref, wo_ref)


def _merge_ffn(x, gates, ysb, yml, ymla, wsb, wml, wmla, wout, g2, wi, wo):
    t, d = x.shape
    tm = min(TM_DENSE, t)
    row = lambda a: pl.BlockSpec((tm, a.shape[1]), lambda i: (i, 0))
    consts = (wsb, wml, wmla, wout, g2, wi, wo)
    return pl.pallas_call(
        _merge_body,
        grid=(t // tm,),
        in_specs=[row(x), row(gates), row(ysb), row(yml), row(ymla)] + [_const_spec(c.shape) for c in consts],
        out_specs=pl.BlockSpec((tm, d), lambda i: (i, 0)),
        out_shape=jax.ShapeDtypeStruct((t, d), F32),
        compiler_params=_cparams(("parallel",)),
        name="merge_ffn",
    )(x, gates, ysb, yml, ymla, *consts)


def _sb_body(q_ref, k_ref, v_ref, o_ref, qm_ref, *, tq):
    qi = pl.program_id(1)
    lane = lax.broadcasted_iota(jnp.int32, (1, LANES), 1)
    qpos = lax.broadcasted_iota(jnp.int32, (tq, tq), 0)
    kpos = lax.broadcasted_iota(jnp.int32, (tq, tq), 1)
    tri = jnp.where(qpos > kpos, 1.0, 0.0).astype(BF16)
    strict = kpos < qpos

    scale = SB_HEAD_DIM ** -0.5
    for h in range(SB_HEADS):
        qp = q_ref[0, :, (h // 2) * LANES:(h // 2 + 1) * LANES]
        lo = (h % 2) * SB_HEAD_DIM
        qm_ref[h] = jnp.where((lane >= lo) & (lane < lo + SB_HEAD_DIM), qp, 0) * scale

    def sweep(kb, carry, masked):
        start = pl.multiple_of(kb * tq, tq)
        new = []
        for h in range(SB_HEADS):
            pair = slice((h // 2) * LANES, (h // 2 + 1) * LANES)
            run, acc = carry[h]
            s = _dot_nt(qm_ref[h], k_ref[0, pl.ds(start, tq), pair])
            ls = jnp.minimum(s, 0.0) - jnp.log(1.0 + jnp.exp(-jnp.abs(s)))
            lk = ls - s
            if masked:
                lk = jnp.where(strict, lk, 0.0)
            w = jnp.exp(ls + _dot(lk.astype(BF16), tri) + run)
            if masked:
                w = jnp.where(strict, w, 0.0)
            acc = acc + _dot(w.astype(BF16), v_ref[0, pl.ds(start, tq), pair])
            run = run + jnp.sum(lk, axis=-1, keepdims=True)
            new.append((run, acc))
        return tuple(new)

    def alive(carry):
        top = functools.reduce(jnp.maximum, [run for run, _ in carry])
        return (jnp.max(top) >= SB_EXP_UNDERFLOW).astype(jnp.int32)

    def body(state):
        kb, _, carry = state
        carry = sweep(kb, carry, False)
        return kb - 1, alive(carry), carry

    init = tuple((jnp.zeros((tq, 1), F32), jnp.zeros((tq, LANES), F32)) for _ in range(SB_HEADS))
    carry = sweep(qi, init, True)
    _, _, carry = lax.while_loop(lambda st: (st[0] >= 0) & (st[1] > 0), body, (qi - 1, alive(carry), carry))
    for p in range(SB_HEADS // 2):
        o_ref[0, :, p * LANES:(p + 1) * LANES] = jnp.where(
            lane < SB_HEAD_DIM, carry[2 * p][1], carry[2 * p + 1][1]).astype(o_ref.dtype)


def _sb_attention(sbqkv):
    b, s, _ = sbqkv.shape
    tq = min(TQ_ATTN, s)
    return pl.pallas_call(
        functools.partial(_sb_body, tq=tq),
        grid=(b, s // tq),
        in_specs=[pl.BlockSpec((1, tq, SB_WIDTH), lambda bi, qi: (bi, qi, 0)),
                  pl.BlockSpec((1, s, SB_WIDTH), lambda bi, qi: (bi, 0, 1)),
                  pl.BlockSpec((1, s, SB_WIDTH), lambda bi, qi: (bi, 0, 2))],
        out_specs=pl.BlockSpec((1, tq, SB_WIDTH), lambda bi, qi: (bi, qi, 0)),
        out_shape=jax.ShapeDtypeStruct((b, s, SB_WIDTH), BF16),
        scratch_shapes=[pltpu.VMEM((SB_HEADS, tq, LANES), BF16)],
        compiler_params=_cparams(("parallel", "arbitrary")),
        name="sb_attention",
    )(sbqkv, sbqkv, sbqkv)


def _split3(x):
    p1 = x.astype(BF16)
    r1 = x - p1.astype(F32)
    p2 = r1.astype(BF16)
    p3 = (r1 - p2.astype(F32)).astype(BF16)
    return p1, p2, p3


def _mlstm_body(qk_ref, vo_ref, if_ref, sh_ref, cw_ref, cb_ref, og_ref, o_ref, ext_ref, c_ref, n_ref, m_ref):
    L = ML_CHUNK

    @pl.when(pl.program_id(1) == 0)
    def _():
        ext_ref[0:L, :] = jnp.zeros((L, 2 * ML_WIDTH), BF16)
        c_ref[...] = jnp.zeros(c_ref.shape, F32)
        n_ref[...] = jnp.zeros(n_ref.shape, F32)
        m_ref[...] = jnp.zeros(m_ref.shape, F32)

    ext_ref[L:, :] = qk_ref[0]
    for sub in range(ML_TILE // L):
        _mlstm_chunk(sub * L, vo_ref, if_ref, sh_ref, cw_ref, cb_ref, og_ref, o_ref, ext_ref, c_ref, n_ref, m_ref)
    ext_ref[0:L, :] = ext_ref[ML_TILE:, :]


def _mlstm_chunk(r0, vo_ref, if_ref, sh_ref, cw_ref, cb_ref, og_ref, o_ref, ext_ref, c_ref, n_ref, m_ref):
    L = ML_CHUNK
    hd = ML_HEAD_DIM
    rows = slice(r0, r0 + L)
    win = ext_ref[r0:r0 + 2 * L, :]
    shifted = _dot(sh_ref[...], win)
    conv = cb_ref[...] + cw_ref[ML_CONV - 1:ML_CONV, :] * win[L:, :].astype(F32)
    for j in range(1, ML_CONV):
        conv += cw_ref[ML_CONV - 1 - j:ML_CONV - j, :] * shifted[(j - 1) * L:j * L, :]
    qk = conv * jax.nn.sigmoid(conv)

    gl = lax.broadcasted_iota(jnp.int32, (1, LANES), 1)
    gin = if_ref[0, rows, :]
    gates = jnp.where(gl < ML_HEADS, gin, jax.nn.log_sigmoid(gin))
    rr = lax.broadcasted_iota(jnp.int32, (L, L), 0)
    cc = lax.broadcasted_iota(jnp.int32, (L, L), 1)
    causal = rr >= cc
    tril = jnp.where(causal, 1.0, 0.0).astype(BF16)
    cum = sum(_dot(tril, p) for p in _split3(gates))
    gates_t = gates.T
    cum_t = cum.T

    for h in range(ML_HEADS):
        qf = qk[:, h * hd:(h + 1) * hd]
        qh = qf.astype(BF16)
        kf = qk[:, ML_WIDTH + h * hd:ML_WIDTH + (h + 1) * hd] * (hd ** -0.5)
        kh = kf.astype(BF16)
        vh = vo_ref[0, rows, h * hd:(h + 1) * hd]
        c_prev = c_ref[h]
        n_prev = n_ref[h:h + 1, :]
        m_prev = m_ref[h:h + 1, 0:1]

        cf_col = cum[:, ML_HEADS + h:ML_HEADS + h + 1]
        ig_col = gates[:, h:h + 1]
        cf_row = cum_t[ML_HEADS + h:ML_HEADS + h + 1, :]
        ig_row = gates_t[h:h + 1, :]

        log_intra = jnp.where(causal, cf_col - cf_row + ig_row, -jnp.inf)
        log_inter = cf_col + m_prev
        m = jnp.maximum(log_inter, jnp.max(log_intra, axis=-1, keepdims=True))
        w_intra = jnp.exp(log_intra - m)
        w_inter = jnp.exp(log_inter - m)
        scores = _dot_nt(qh, kh) * w_intra
        num = _dot(scores.astype(BF16), vh) + w_inter * _dot(qh, c_prev.astype(BF16))
        qn = jnp.sum(qf * n_prev, axis=-1, keepdims=True)
        den = jnp.sum(scores, axis=-1, keepdims=True) + w_inter * qn
        hout = num / jnp.maximum(jnp.abs(den), jnp.exp(-m))

        f_total = cf_col[L - 1:L, :]
        log_to_end = f_total - cf_col + ig_col
        m_new = jnp.maximum(f_total + m_prev, jnp.max(log_to_end, axis=0, keepdims=True))
        decay = jnp.exp(f_total + m_prev - m_new)
        w_end = jnp.exp(log_to_end - m_new)
        kw = kf * w_end
        c_ref[h] = decay * c_prev + _dot_tn(kw.astype(BF16), vh)
        n_ref[h:h + 1, :] = decay * n_prev + jnp.sum(kw, axis=0, keepdims=True)
        m_ref[h:h + 1, :] = jnp.broadcast_to(m_new, (1, LANES))

        hn = _rms(hout, og_ref[:, h * hd:(h + 1) * hd])
        o_pre = vo_ref[0, rows, ML_WIDTH + h * hd:ML_WIDTH + (h + 1) * hd].astype(F32)
        o_ref[0, rows, h * hd:(h + 1) * hd] = (jax.nn.sigmoid(o_pre) * hn).astype(o_ref.dtype)


def _conv_shift_matrix():
    L = ML_CHUNK
    sh = np.zeros(((ML_CONV - 1) * L, 2 * L), np.float32)
    for j in range(1, ML_CONV):
        sh[(j - 1) * L + np.arange(L), L + np.arange(L) - j] = 1.0
    return jnp.asarray(sh, BF16)


def _mlstm(mlqk, mlvo, gif, conv_w, conv_b, out_gain):
    b, s, _ = mlqk.shape
    tl = ML_TILE
    assert s % tl == 0
    shift = _conv_shift_matrix()
    blk = lambda w: pl.BlockSpec((1, tl, w), lambda bi, ci: (bi, ci, 0))
    return pl.pallas_call(
        _mlstm_body,
        grid=(b, s // tl),
        in_specs=[blk(W_MLQK), blk(W_MLVO), blk(W_IF), _const_spec(shift.shape),
                  _const_spec(conv_w.shape), _const_spec(conv_b.shape), _const_spec(out_gain.shape)],
        out_specs=blk(ML_WIDTH),
        out_shape=jax.ShapeDtypeStruct((b, s, ML_WIDTH), BF16),
        scratch_shapes=[pltpu.VMEM((ML_CHUNK + tl, W_MLQK), BF16),
                        pltpu.VMEM((ML_HEADS, ML_HEAD_DIM, ML_HEAD_DIM), F32),
                        pltpu.VMEM((8, ML_HEAD_DIM), F32),
                        pltpu.VMEM((8, LANES), F32)],
        compiler_params=_cparams(("parallel", "arbitrary")),
        name="mlstm",
    )(mlqk, mlvo, gif, shift, conv_w, conv_b, out_gain)


def _rope_table_body(pos_ref, invf_ref, sign_ref, cos_ref, sin_ref):
    ang = pos_ref[...].astype(F32) * invf_ref[...]
    cos_ref[...] = jnp.cos(ang)
    sin_ref[...] = jnp.sin(ang) * sign_ref[...]


def _rope_tables(positions):
    t = positions.size
    half = MLA_ROPE // 2
    inv_freq = jnp.power(ROPE_THETA, -jnp.arange(half, dtype=F32) / half)
    zeros = jnp.zeros((half,), F32)
    pad = jnp.zeros((LANES - MLA_QK,), F32)
    nope = jnp.zeros((MLA_NOPE,), F32)
    invf = jnp.concatenate([nope, inv_freq, inv_freq, pad])[None, :]
    sign = jnp.concatenate([nope, zeros - 1.0, zeros + 1.0, pad])[None, :]
    tm = min(TM_DENSE, t)
    return pl.pallas_call(
        _rope_table_body,
        grid=(t // tm,),
        in_specs=[pl.BlockSpec((tm, 1), lambda i: (i, 0)), _const_spec(invf.shape), _const_spec(sign.shape)],
        out_specs=[pl.BlockSpec((tm, LANES), lambda i: (i, 0))] * 2,
        out_shape=[jax.ShapeDtypeStruct((t, LANES), F32)] * 2,
        compiler_params=_cparams(("parallel",)),
        name="rope_tables",
    )(positions.reshape(t, 1), invf, sign)


def _mla_prep_body(c_ref, cos_ref, sin_ref, qn_ref, kvn_ref, wq_ref, wk_ref, wv_ref, qg_ref, kg_ref,
                   q_out, k_out, v_out):
    c = c_ref[...].astype(F32)
    cq = _rms(c[:, :MLA_Q_RANK], qn_ref[...]).astype(BF16)
    ckv = _rms(c[:, MLA_Q_RANK:MLA_Q_RANK + MLA_KV_RANK], kvn_ref[...]).astype(BF16)
    k_rope = c[:, MLA_Q_RANK + MLA_KV_RANK:]
    q = _dot(cq, wq_ref[...])
    k = _dot(ckv, wk_ref[...])
    vlane = lax.broadcasted_iota(jnp.int32, (1, MLA_HEADS * LANES), 1)
    ones_half = jnp.where(vlane % LANES >= MLA_V, 1.0, 0.0)
    v_out[...] = (_dot(ckv, wv_ref[...]) + ones_half).astype(v_out.dtype)
    cos = cos_ref[...]
    sin = sin_ref[...]
    lane = lax.broadcasted_iota(jnp.int32, (1, LANES), 1)
    first_half = lane < MLA_NOPE + MLA_ROPE // 2

    def norm_rope(x, gain):
        y = x * lax.rsqrt(jnp.sum(x * x, axis=-1, keepdims=True) * (1.0 / MLA_QK) + EPS) * gain
        rot = jnp.where(first_half, pltpu.roll(y, LANES - MLA_ROPE // 2, 1), pltpu.roll(y, MLA_ROPE // 2, 1))
        return y * cos + rot * sin

    for h in range(MLA_HEADS):
        sl = slice(h * LANES, (h + 1) * LANES)
        q_out[:, sl] = norm_rope(q[:, sl], qg_ref[...]).astype(q_out.dtype)
        k_out[:, sl] = norm_rope(k[:, sl] + k_rope, kg_ref[...]).astype(k_out.dtype)


def _mla_prep(mla_c, cos, sin, qn, kvn, wq, wk, wv, qg, kg):
    t = mla_c.shape[0]
    tm = min(TM_DENSE, t)
    row = lambda w: pl.BlockSpec((tm, w), lambda i: (i, 0))
    consts = (qn, kvn, wq, wk, wv, qg, kg)
    widths = (MLA_HEADS * LANES,) * 3
    return pl.pallas_call(
        _mla_prep_body,
        grid=(t // tm,),
        in_specs=[row(W_MLA), row(LANES), row(LANES)] + [_const_spec(c.shape) for c in consts],
        out_specs=[row(w) for w in widths],
        out_shape=[jax.ShapeDtypeStruct((t, w), BF16) for w in widths],
        compiler_params=_cparams(("parallel",)),
        name="mla_prep",
    )(mla_c, cos, sin, *consts)


def _mla_attn_body(q_ref, k_ref, v_ref, o_ref, sa_ref, sb_ref, pa_ref, pb_ref, ala_ref, alb_ref, m_ref, acc_ref):
    qi = pl.program_id(2)
    tq, tk, rc = MLA_TQ, MLA_TK, MLA_ROWS
    m_ref[...] = jnp.full(m_ref.shape, -jnp.inf, F32)
    acc_ref[...] = jnp.zeros(acc_ref.shape, F32)
    pb_ref[...] = jnp.zeros(pb_ref.shape, BF16)
    alb_ref[...] = jnp.ones(alb_ref.shape, F32)

    def scores(blk, s_ref, row0=0):
        start = pl.multiple_of(blk * tk, tk)
        for h in range(2):
            hl = slice(h * LANES, (h + 1) * LANES)
            s_ref[h, row0:, :] = _dot_nt(q_ref[0, row0:, hl], k_ref[0, pl.ds(start, tk), hl])

    def softmax(s_ref, p_ref, al_ref, diag_block=None, row0=0):
        for h in range(2):
            for r in range(row0, tq, rc):
                s = s_ref[h, r:r + rc, :]
                if diag_block is not None and diag_block * tk + tk - 1 > r:
                    qpos = lax.broadcasted_iota(jnp.int32, (rc, tk), 0) + r
                    kpos = lax.broadcasted_iota(jnp.int32, (rc, tk), 1) + diag_block * tk
                    s = jnp.where(kpos <= qpos, s, -jnp.inf)
                m_old = m_ref[h, r:r + rc, :]
                m_new = jnp.maximum(m_old, jnp.max(s, axis=-1, keepdims=True))
                al_ref[h, r:r + rc, :] = jnp.exp2(m_old - m_new)
                m_ref[h, r:r + rc, :] = m_new
                for c in range(0, tk, LANES):
                    p_ref[h, r:r + rc, c:c + LANES] = jnp.exp2(s[:, c:c + LANES] - m_new).astype(BF16)

    def weighted_values(blk, p_ref, al_ref, row0=0):
        start = pl.multiple_of(blk * tk, tk)
        for h in range(2):
            hl = slice(h * LANES, (h + 1) * LANES)
            acc_ref[h, row0:, :] = (al_ref[h, row0:, :] * acc_ref[h, row0:, :]
                                    + _dot(p_ref[h, row0:, :], v_ref[0, pl.ds(start, tk), hl]))

    scores(0, sa_ref)

    def block_pair(j, carry):
        weighted_values(jnp.maximum(2 * j - 1, 0), pb_ref, alb_ref)
        scores(2 * j + 1, sb_ref)
        softmax(sa_ref, pa_ref, ala_ref)
        weighted_values(2 * j, pa_ref, ala_ref)
        scores(2 * j + 2, sa_ref)
        softmax(sb_ref, pb_ref, alb_ref)
        return carry

    lax.fori_loop(0, qi, block_pair, 0)
    weighted_values(jnp.maximum(2 * qi - 1, 0), pb_ref, alb_ref)
    scores(2 * qi + 1, sb_ref, row0=tk)
    softmax(sa_ref, pa_ref, ala_ref, diag_block=0)
    weighted_values(2 * qi, pa_ref, ala_ref)
    softmax(sb_ref, pb_ref, alb_ref, diag_block=1, row0=tk)
    weighted_values(2 * qi + 1, pb_ref, alb_ref, row0=tk)

    lane = lax.broadcasted_iota(jnp.int32, (1, LANES), 1)
    a0, a1 = acc_ref[0], acc_ref[1]
    o0 = a0 * pltpu.roll(1.0 / a0, MLA_V, 1)
    o1 = pltpu.roll(a1, MLA_V, 1) * (1.0 / a1)
    o_ref[0] = jnp.where(lane < MLA_V, o0, o1).astype(o_ref.dtype)


def _mla_attention(q, k, v):
    b, s, _ = q.shape
    tq = MLA_TQ
    assert s % tq == 0
    npair = MLA_HEADS // 2
    return pl.pallas_call(
        _mla_attn_body,
        grid=(b, npair, s // tq),
        in_specs=[pl.BlockSpec((1, tq, 2 * LANES), lambda bi, hp, qi: (bi, qi, hp)),
                  pl.BlockSpec((1, s, 2 * LANES), lambda bi, hp, qi: (bi, 0, hp)),
                  pl.BlockSpec((1, s, 2 * LANES), lambda bi, hp, qi: (bi, 0, hp))],
        out_specs=pl.BlockSpec((1, tq, LANES), lambda bi, hp, qi: (bi, qi, hp)),
        out_shape=jax.ShapeDtypeStruct((b, s, MLA_WIDTH), BF16),
        scratch_shapes=[pltpu.VMEM((2, tq, MLA_TK), F32), pltpu.VMEM((2, tq, MLA_TK), F32),
                        pltpu.VMEM((2, tq, MLA_TK), BF16), pltpu.VMEM((2, tq, MLA_TK), BF16),
                        pltpu.VMEM((2, tq, LANES), F32), pltpu.VMEM((2, tq, LANES), F32),
                        pltpu.VMEM((2, tq, LANES), F32), pltpu.VMEM((2, tq, LANES), F32)],
        compiler_params=_cparams(("parallel", "parallel", "arbitrary")),
        name="mla_attention",
    )(q, k, v)


def _seg(z, name):
    off = 0
    for n, w in SEGMENTS:
        if n == name:
            return z[..., off:off + w]
        off += w
    raise KeyError(name)


def _pack_inproj(w, b):
    wb = jnp.concatenate([w, b[None, :]], axis=0)
    z = lambda n: jnp.zeros((wb.shape[0], n), wb.dtype)
    kr = jnp.concatenate([z(MLA_NOPE), _seg(wb, 'mla_kr'), z(LANES - MLA_QK)], axis=1)
    gif = jnp.concatenate([_seg(wb, 'ml_i'), _seg(wb, 'ml_f'), z(LANES - 2 * ML_HEADS)], axis=1)
    packed = jnp.concatenate(
        [_seg(wb, 'gates'), _seg(wb, 'ml_q'), _seg(wb, 'ml_k'), _seg(wb, 'ml_v'), _seg(wb, 'ml_o'),
         _seg(wb, 'sb_q'), _seg(wb, 'sb_k'), _seg(wb, 'sb_v'), _seg(wb, 'mla_cq'), _seg(wb, 'mla_ckv'),
         kr, gif], axis=1)
    return packed[:-1].astype(BF16), packed[-1:]


def _pad_heads(w, width, lo, hi):
    r = w.shape[0]
    wh = w.reshape(r, MLA_HEADS, width)[:, :, lo:hi]
    return jnp.pad(wh, ((0, 0), (0, 0), (0, LANES - (hi - lo)))).reshape(r, MLA_HEADS * LANES)


def kernel(x, positions, ffn1_norm, ffn1_wi, ffn1_wo, mix_norm, w_in, b_in, ml_conv_w, ml_conv_b,
           ml_out_norm, mla_q_norm, mla_kv_norm, mla_wq_up, mla_wkv_up, mla_q_gain, mla_k_gain,
           w_up_sb, w_up_ml, w_up_mla, w_out, ffn2_norm, ffn2_wi, ffn2_wo):
    b, s, d = x.shape
    t = b * s
    depth = w_in.shape[0]
    cos, sin = _rope_tables(positions)
    xt = x.reshape(t, d)
    row = lambda a: a[None, :]
    pad_gain = lambda g: jnp.pad(g, (0, LANES - MLA_QK))[None, :]
    for l in range(depth):
        xt = _ffn(xt, row(ffn1_norm[l]), ffn1_wi[l].astype(BF16), ffn1_wo[l].astype(BF16))

        w_packed, b_packed = _pack_inproj(w_in[l], b_in[l])
        gates, mlqk, mlvo, sbqkv, mla_c, gif = _inproj(xt, row(mix_norm[l]), w_packed, b_packed)

        ysb = _sb_attention(sbqkv.reshape(b, s, W_SB))

        yml = _mlstm(mlqk.reshape(b, s, W_MLQK), mlvo.reshape(b, s, W_MLVO), gif.reshape(b, s, W_IF),
                     ml_conv_w[l], row(ml_conv_b[l]), row(ml_out_norm[l]))

        wkv = mla_wkv_up[l]
        kvw = MLA_NOPE + MLA_V
        q_scale = MLA_QK ** -0.5 * np.log2(np.e)
        qh, kh, vh = _mla_prep(
            mla_c, cos, sin, row(mla_q_norm[l]), row(mla_kv_norm[l]),
            _pad_heads(mla_wq_up[l], MLA_QK, 0, MLA_QK).astype(BF16),
            _pad_heads(wkv, kvw, 0, MLA_NOPE).astype(BF16), _pad_heads(wkv, kvw, MLA_NOPE, kvw).astype(BF16),
            pad_gain(mla_q_gain[l] * q_scale), pad_gain(mla_k_gain[l]))
        ymla = _mla_attention(qh.reshape(b, s, -1), kh.reshape(b, s, -1), vh.reshape(b, s, -1))

        xt = _merge_ffn(xt, gates, ysb.reshape(t, -1), yml.reshape(t, -1), ymla.reshape(t, -1),
                        w_up_sb[l].astype(BF16), w_up_ml[l].astype(BF16), w_up_mla[l].astype(BF16),
                        w_out[l].astype(BF16), row(ffn2_norm[l]), ffn2_wi[l].astype(BF16), ffn2_wo[l].astype(BF16))
    return xt.reshape(b, s, d)
```

```python
import functools

import numpy as np
import jax
import jax.numpy as jnp
from jax import lax
from jax.experimental import pallas as pl
from jax.experimental.pallas import tpu as pltpu

F32 = jnp.float32
BF16 = jnp.bfloat16

EPS = 1e-6
LANES = 128
V7X_VMEM_LIMIT = 56 * 1024 * 1024

D_MODEL = 1024
FFN_DIM = 1408
SB_HEADS, SB_HEAD_DIM = 4, 64
SB_WIDTH = SB_HEADS * SB_HEAD_DIM
ML_HEADS, ML_HEAD_DIM = 4, 128
ML_WIDTH = ML_HEADS * ML_HEAD_DIM
ML_CHUNK = 128
ML_CONV = 4
MLA_HEADS, MLA_NOPE, MLA_ROPE, MLA_V = 4, 64, 32, 64
MLA_QK = MLA_NOPE + MLA_ROPE
MLA_WIDTH = MLA_HEADS * MLA_V
MLA_Q_RANK, MLA_KV_RANK = 256, 128
ROPE_THETA = 10000.0
N_BRANCH = 3

SEGMENTS = (
    ('sb_q', SB_WIDTH), ('sb_k', SB_WIDTH), ('sb_v', SB_WIDTH),
    ('ml_q', ML_WIDTH), ('ml_k', ML_WIDTH), ('ml_v', ML_WIDTH), ('ml_o', ML_WIDTH),
    ('ml_i', ML_HEADS), ('ml_f', ML_HEADS),
    ('mla_cq', MLA_Q_RANK), ('mla_ckv', MLA_KV_RANK), ('mla_kr', MLA_ROPE),
    ('gates', N_BRANCH * D_MODEL),
)

W_GATES = N_BRANCH * D_MODEL
W_MLQK = 2 * ML_WIDTH
W_MLVO = 2 * ML_WIDTH
W_SB = 3 * SB_WIDTH
W_MLA = MLA_Q_RANK + MLA_KV_RANK + LANES
W_IF = LANES
OUT_WIDTHS = (W_GATES, W_MLQK, W_MLVO, W_SB, W_MLA, W_IF)
N_PACKED = sum(OUT_WIDTHS)

TM_DENSE = 512
TQ_ATTN = 256
ML_TILE = 256
MLA_TQ = 1024
MLA_TK = 512
MLA_ROWS = 32
SB_EXP_UNDERFLOW = -120.0


def _cparams(sem):
    return pltpu.CompilerParams(dimension_semantics=sem, vmem_limit_bytes=V7X_VMEM_LIMIT)


def _const_spec(shape):
    nd = len(shape)
    return pl.BlockSpec(shape, lambda *_: (0,) * nd)


def _rms(x, gain):
    return x * lax.rsqrt(jnp.mean(x * x, axis=-1, keepdims=True) + EPS) * gain


def _dot(a, b):
    return jnp.dot(a, b, preferred_element_type=F32)


def _dot_nt(a, b):
    return lax.dot_general(a, b, (((1,), (1,)), ((), ())), preferred_element_type=F32)


def _dot_tn(a, b):
    return lax.dot_general(a, b, (((0,), (0,)), ((), ())), preferred_element_type=F32)


def _swiglu_residual(x, gain, wi_ref, wo_ref):
    u = _rms(x, gain).astype(BF16)
    h = _dot(u, wi_ref[...])
    a = h[:, :FFN_DIM]
    g = h[:, FFN_DIM:]
    act = (a * jax.nn.sigmoid(a) * g).astype(BF16)
    return x + 0.5 * _dot(act, wo_ref[...])


def _ffn_body(x_ref, g_ref, wi_ref, wo_ref, o_ref):
    o_ref[...] = _swiglu_residual(x_ref[...], g_ref[...], wi_ref, wo_ref)


def _ffn(x, gain, wi, wo):
    t, d = x.shape
    tm = min(TM_DENSE, t)
    return pl.pallas_call(
        _ffn_body,
        grid=(t // tm,),
        in_specs=[pl.BlockSpec((tm, d), lambda i: (i, 0)),
                  _const_spec(gain.shape), _const_spec(wi.shape), _const_spec(wo.shape)],
        out_specs=pl.BlockSpec((tm, d), lambda i: (i, 0)),
        out_shape=jax.ShapeDtypeStruct((t, d), F32),
        compiler_params=_cparams(("parallel",)),
        name="ffn",
    )(x, gain, wi, wo)


def _inproj_body(x_ref, g_ref, w_ref, b_ref, *out_refs):
    u = _rms(x_ref[...], g_ref[...]).astype(BF16)
    off = 0
    for o_ref, width in zip(out_refs, OUT_WIDTHS):
        step = min(width, 1024)
        for c in range(0, width, step):
            z = _dot(u, w_ref[:, off + c:off + c + step]) + b_ref[:, off + c:off + c + step]
            o_ref[:, c:c + step] = z.astype(o_ref.dtype)
        off += width


def _inproj(x, gain, w, b):
    t, d = x.shape
    tm = min(TM_DENSE, t)
    dtypes = (BF16, BF16, BF16, BF16, BF16, F32)
    return pl.pallas_call(
        _inproj_body,
        grid=(t // tm,),
        in_specs=[pl.BlockSpec((tm, d), lambda i: (i, 0)),
                  _const_spec(gain.shape), _const_spec(w.shape), _const_spec(b.shape)],
        out_specs=[pl.BlockSpec((tm, wd), lambda i: (i, 0)) for wd in OUT_WIDTHS],
        out_shape=[jax.ShapeDtypeStruct((t, wd), dt) for wd, dt in zip(OUT_WIDTHS, dtypes)],
        compiler_params=_cparams(("parallel",)),
        name="inproj",
    )(x, gain, w, b)


def _merge_body(x_ref, gt_ref, ysb_ref, yml_ref, ymla_ref, wsb_ref, wml_ref, wmla_ref, wout_ref,
                g2_ref, wi_ref, wo_ref, o_ref):
    d = D_MODEL
    merged = jax.nn.sigmoid(gt_ref[:, 0:d].astype(F32)) * _dot(ysb_ref[...], wsb_ref[...])
    merged += jax.nn.sigmoid(gt_ref[:, d:2 * d].astype(F32)) * _dot(yml_ref[...], wml_ref[...])
    merged += jax.nn.sigmoid(gt_ref[:, 2 * d:3 * d].astype(F32)) * _dot(ymla_ref[...], wmla_ref[...])
    x = x_ref[...] + _dot(merged.astype(BF16), wout_ref[...])
    o_ref[...] = _swiglu_residual(x, g2_ref[...], wi_ref, wo_ref)


def _merge_ffn(x, gates, ysb, yml, ymla, wsb, wml, wmla, wout, g2, wi, wo):
    t, d = x.shape
    tm = min(TM_DENSE, t)
    row = lambda a: pl.BlockSpec((tm, a.shape[1]), lambda i: (i, 0))
    consts = (wsb, wml, wmla, wout, g2, wi, wo)
    return pl.pallas_call(
        _merge_body,
        grid=(t // tm,),
        in_specs=[row(x), row(gates), row(ysb), row(yml), row(ymla)] + [_const_spec(c.shape) for c in consts],
        out_specs=pl.BlockSpec((tm, d), lambda i: (i, 0)),
        out_shape=jax.ShapeDtypeStruct((t, d), F32),
        compiler_params=_cparams(("parallel",)),
        name="merge_ffn",
    )(x, gates, ysb, yml, ymla, *consts)


def _sb_body(q_ref, k_ref, v_ref, o_ref, qm_ref, *, tq):
    qi = pl.program_id(1)
    lane = lax.broadcasted_iota(jnp.int32, (1, LANES), 1)
    qpos = lax.broadcasted_iota(jnp.int32, (tq, tq), 0)
    kpos = lax.broadcasted_iota(jnp.int32, (tq, tq), 1)
    tri = jnp.where(qpos > kpos, 1.0, 0.0).astype(BF16)
    strict = kpos < qpos

    scale = SB_HEAD_DIM ** -0.5
    for h in range(SB_HEADS):
        qp = q_ref[0, :, (h // 2) * LANES:(h // 2 + 1) * LANES]
        lo = (h % 2) * SB_HEAD_DIM
        qm_ref[h] = jnp.where((lane >= lo) & (lane < lo + SB_HEAD_DIM), qp, 0) * scale

    def sweep(kb, carry, masked):
        start = pl.multiple_of(kb * tq, tq)
        new = []
        for h in range(SB_HEADS):
            pair = slice((h // 2) * LANES, (h // 2 + 1) * LANES)
            run, acc = carry[h]
            s = _dot_nt(qm_ref[h], k_ref[0, pl.ds(start, tq), pair])
            ls = jnp.minimum(s, 0.0) - jnp.log(1.0 + jnp.exp(-jnp.abs(s)))
            lk = ls - s
            if masked:
                lk = jnp.where(strict, lk, 0.0)
            w = jnp.exp(ls + _dot(lk.astype(BF16), tri) + run)
            if masked:
                w = jnp.where(strict, w, 0.0)
            acc = acc + _dot(w.astype(BF16), v_ref[0, pl.ds(start, tq), pair])
            run = run + jnp.sum(lk, axis=-1, keepdims=True)
            new.append((run, acc))
        return tuple(new)

    def alive(carry):
        top = functools.reduce(jnp.maximum, [run for run, _ in carry])
        return (jnp.max(top) >= SB_EXP_UNDERFLOW).astype(jnp.int32)

    def body(state):
        kb, _, carry = state
        carry = sweep(kb, carry, False)
        return kb - 1, alive(carry), carry

    init = tuple((jnp.zeros((tq, 1), F32), jnp.zeros((tq, LANES), F32)) for _ in range(SB_HEADS))
    carry = sweep(qi, init, True)
    _, _, carry = lax.while_loop(lambda st: (st[0] >= 0) & (st[1] > 0), body, (qi - 1, alive(carry), carry))
    for p in range(SB_HEADS // 2):
        o_ref[0, :, p * LANES:(p + 1) * LANES] = jnp.where(
            lane < SB_HEAD_DIM, carry[2 * p][1], carry[2 * p + 1][1]).astype(o_ref.dtype)


def _sb_attention(sbqkv):
    b, s, _ = sbqkv.shape
    tq = min(TQ_ATTN, s)
    return pl.pallas_call(
        functools.partial(_sb_body, tq=tq),
        grid=(b, s // tq),
        in_specs=[pl.BlockSpec((1, tq, SB_WIDTH), lambda bi, qi: (bi, qi, 0)),
                  pl.BlockSpec((1, s, SB_WIDTH), lambda bi, qi: (bi, 0, 1)),
                  pl.BlockSpec((1, s, SB_WIDTH), lambda bi, qi: (bi, 0, 2))],
        out_specs=pl.BlockSpec((1, tq, SB_WIDTH), lambda bi, qi: (bi, qi, 0)),
        out_shape=jax.ShapeDtypeStruct((b, s, SB_WIDTH), BF16),
        scratch_shapes=[pltpu.VMEM((SB_HEADS, tq, LANES), BF16)],
        compiler_params=_cparams(("parallel", "arbitrary")),
        name="sb_attention",
    )(sbqkv, sbqkv, sbqkv)


def _split3(x):
    p1 = x.astype(BF16)
    r1 = x - p1.astype(F32)
    p2 = r1.astype(BF16)
    p3 = (r1 - p2.astype(F32)).astype(BF16)
    return p1, p2, p3


def _mlstm_body(qk_ref, vo_ref, if_ref, sh_ref, cw_ref, cb_ref, og_ref, o_ref, ext_ref, c_ref, n_ref, m_ref):
    L = ML_CHUNK

    @pl.when(pl.program_id(1) == 0)
    def _():
        ext_ref[0:L, :] = jnp.zeros((L, 2 * ML_WIDTH), BF16)
        c_ref[...] = jnp.zeros(c_ref.shape, F32)
        n_ref[...] = jnp.zeros(n_ref.shape, F32)
        m_ref[...] = jnp.zeros(m_ref.shape, F32)

    ext_ref[L:, :] = qk_ref[0]
    for sub in range(ML_TILE // L):
        _mlstm_chunk(sub * L, vo_ref, if_ref, sh_ref, cw_ref, cb_ref, og_ref, o_ref, ext_ref, c_ref, n_ref, m_ref)
    ext_ref[0:L, :] = ext_ref[ML_TILE:, :]


def _mlstm_chunk(r0, vo_ref, if_ref, sh_ref, cw_ref, cb_ref, og_ref, o_ref, ext_ref, c_ref, n_ref, m_ref):
    L = ML_CHUNK
    hd = ML_HEAD_DIM
    rows = slice(r0, r0 + L)
    win = ext_ref[r0:r0 + 2 * L, :]
    shifted = _dot(sh_ref[...], win)
    conv = cb_ref[...] + cw_ref[ML_CONV - 1:ML_CONV, :] * win[L:, :].astype(F32)
    for j in range(1, ML_CONV):
        conv += cw_ref[ML_CONV - 1 - j:ML_CONV - j, :] * shifted[(j - 1) * L:j * L, :]
    qk = conv * jax.nn.sigmoid(conv)

    gl = lax.broadcasted_iota(jnp.int32, (1, LANES), 1)
    gin = if_ref[0, rows, :]
    gates = jnp.where(gl < ML_HEADS, gin, jax.nn.log_sigmoid(gin))
    rr = lax.broadcasted_iota(jnp.int32, (L, L), 0)
    cc = lax.broadcasted_iota(jnp.int32, (L, L), 1)
    causal = rr >= cc
    tril = jnp.where(causal, 1.0, 0.0).astype(BF16)
    cum = sum(_dot(tril, p) for p in _split3(gates))
    gates_t = gates.T
    cum_t = cum.T

    for h in range(ML_HEADS):
        qf = qk[:, h * hd:(h + 1) * hd]
        qh = qf.astype(BF16)
        kf = qk[:, ML_WIDTH + h * hd:ML_WIDTH + (h + 1) * hd] * (hd ** -0.5)
        kh = kf.astype(BF16)
        vh = vo_ref[0, rows, h * hd:(h + 1) * hd]
        c_prev = c_ref[h]
        n_prev = n_ref[h:h + 1, :]
        m_prev = m_ref[h:h + 1, 0:1]

        cf_col = cum[:, ML_HEADS + h:ML_HEADS + h + 1]
        ig_col = gates[:, h:h + 1]
        cf_row = cum_t[ML_HEADS + h:ML_HEADS + h + 1, :]
        ig_row = gates_t[h:h + 1, :]

        log_intra = jnp.where(causal, cf_col - cf_row + ig_row, -jnp.inf)
        log_inter = cf_col + m_prev
        m = jnp.maximum(log_inter, jnp.max(log_intra, axis=-1, keepdims=True))
        w_intra = jnp.exp(log_intra - m)
        w_inter = jnp.exp(log_inter - m)
        scores = _dot_nt(qh, kh) * w_intra
        num = _dot(scores.astype(BF16), vh) + w_inter * _dot(qh, c_prev.astype(BF16))
        qn = jnp.sum(qf * n_prev, axis=-1, keepdims=True)
        den = jnp.sum(scores, axis=-1, keepdims=True) + w_inter * qn
        hout = num / jnp.maximum(jnp.abs(den), jnp.exp(-m))

        f_total = cf_col[L - 1:L, :]
        log_to_end = f_total - cf_col + ig_col
        m_new = jnp.maximum(f_total + m_prev, jnp.max(log_to_end, axis=0, keepdims=True))
        decay = jnp.exp(f_total + m_prev - m_new)
        w_end = jnp.exp(log_to_end - m_new)
        kw = kf * w_end
        c_ref[h] = decay * c_prev + _dot_tn(kw.astype(BF16), vh)
        n_ref[h:h + 1, :] = decay * n_prev + jnp.sum(kw, axis=0, keepdims=True)
        m_ref[h:h + 1, :] = jnp.broadcast_to(m_new, (1, LANES))

        hn = _rms(hout, og_ref[:, h * hd:(h + 1) * hd])
        o_pre = vo_ref[0, rows, ML_WIDTH + h * hd:ML_WIDTH + (h + 1) * hd].astype(F32)
        o_ref[0, rows, h * hd:(h + 1) * hd] = (jax.nn.sigmoid(o_pre) * hn).astype(o_ref.dtype)


def _conv_shift_matrix():
    L = ML_CHUNK
    sh = np.zeros(((ML_CONV - 1) * L, 2 * L), np.float32)
    for j in range(1, ML_CONV):
        sh[(j - 1) * L + np.arange(L), L + np.arange(L) - j] = 1.0
    return jnp.asarray(sh, BF16)


def _mlstm(mlqk, mlvo, gif, conv_w, conv_b, out_gain):
    b, s, _ = mlqk.shape
    tl = ML_TILE
    assert s % tl == 0
    shift = _conv_shift_matrix()
    blk = lambda w: pl.BlockSpec((1, tl, w), lambda bi, ci: (bi, ci, 0))
    return pl.pallas_call(
        _mlstm_body,
        grid=(b, s // tl),
        in_specs=[blk(W_MLQK), blk(W_MLVO), blk(W_IF), _const_spec(shift.shape),
                  _const_spec(conv_w.shape), _const_spec(conv_b.shape), _const_spec(out_gain.shape)],
        out_specs=blk(ML_WIDTH),
        out_shape=jax.ShapeDtypeStruct((b, s, ML_WIDTH), BF16),
        scratch_shapes=[pltpu.VMEM((ML_CHUNK + tl, W_MLQK), BF16),
                        pltpu.VMEM((ML_HEADS, ML_HEAD_DIM, ML_HEAD_DIM), F32),
                        pltpu.VMEM((8, ML_HEAD_DIM), F32),
                        pltpu.VMEM((8, LANES), F32)],
        compiler_params=_cparams(("parallel", "arbitrary")),
        name="mlstm",
    )(mlqk, mlvo, gif, shift, conv_w, conv_b, out_gain)


def _rope_table_body(pos_ref, invf_ref, sign_ref, cos_ref, sin_ref):
    ang = pos_ref[...].astype(F32) * invf_ref[...]
    cos_ref[...] = jnp.cos(ang)
    sin_ref[...] = jnp.sin(ang) * sign_ref[...]


def _rope_tables(positions):
    t = positions.size
    half = MLA_ROPE // 2
    inv_freq = jnp.power(ROPE_THETA, -jnp.arange(half, dtype=F32) / half)
    zeros = jnp.zeros((half,), F32)
    pad = jnp.zeros((LANES - MLA_QK,), F32)
    nope = jnp.zeros((MLA_NOPE,), F32)
    invf = jnp.concatenate([nope, inv_freq, inv_freq, pad])[None, :]
    sign = jnp.concatenate([nope, zeros - 1.0, zeros + 1.0, pad])[None, :]
    tm = min(TM_DENSE, t)
    return pl.pallas_call(
        _rope_table_body,
        grid=(t // tm,),
        in_specs=[pl.BlockSpec((tm, 1), lambda i: (i, 0)), _const_spec(invf.shape), _const_spec(sign.shape)],
        out_specs=[pl.BlockSpec((tm, LANES), lambda i: (i, 0))] * 2,
        out_shape=[jax.ShapeDtypeStruct((t, LANES), F32)] * 2,
        compiler_params=_cparams(("parallel",)),
        name="rope_tables",
    )(positions.reshape(t, 1), invf, sign)


def _mla_prep_body(c_ref, cos_ref, sin_ref, qn_ref, kvn_ref, wq_ref, wk_ref, wv_ref, qg_ref, kg_ref,
                   q_out, k_out, v_out):
    c = c_ref[...].astype(F32)
    cq = _rms(c[:, :MLA_Q_RANK], qn_ref[...]).astype(BF16)
    ckv = _rms(c[:, MLA_Q_RANK:MLA_Q_RANK + MLA_KV_RANK], kvn_ref[...]).astype(BF16)
    k_rope = c[:, MLA_Q_RANK + MLA_KV_RANK:]
    q = _dot(cq, wq_ref[...])
    k = _dot(ckv, wk_ref[...])
    vlane = lax.broadcasted_iota(jnp.int32, (1, MLA_HEADS * LANES), 1)
    ones_half = jnp.where(vlane % LANES >= MLA_V, 1.0, 0.0)
    v_out[...] = (_dot(ckv, wv_ref[...]) + ones_half).astype(v_out.dtype)
    cos = cos_ref[...]
    sin = sin_ref[...]
    lane = lax.broadcasted_iota(jnp.int32, (1, LANES), 1)
    first_half = lane < MLA_NOPE + MLA_ROPE // 2

    def norm_rope(x, gain):
        y = x * lax.rsqrt(jnp.sum(x * x, axis=-1, keepdims=True) * (1.0 / MLA_QK) + EPS) * gain
        rot = jnp.where(first_half, pltpu.roll(y, LANES - MLA_ROPE // 2, 1), pltpu.roll(y, MLA_ROPE // 2, 1))
        return y * cos + rot * sin

    for h in range(MLA_HEADS):
        sl = slice(h * LANES, (h + 1) * LANES)
        q_out[:, sl] = norm_rope(q[:, sl], qg_ref[...]).astype(q_out.dtype)
        k_out[:, sl] = norm_rope(k[:, sl] + k_rope, kg_ref[...]).astype(k_out.dtype)


def _mla_prep(mla_c, cos, sin, qn, kvn, wq, wk, wv, qg, kg):
    t = mla_c.shape[0]
    tm = min(TM_DENSE, t)
    row = lambda w: pl.BlockSpec((tm, w), lambda i: (i, 0))
    consts = (qn, kvn, wq, wk, wv, qg, kg)
    widths = (MLA_HEADS * LANES,) * 3
    return pl.pallas_call(
        _mla_prep_body,
        grid=(t // tm,),
        in_specs=[row(W_MLA), row(LANES), row(LANES)] + [_const_spec(c.shape) for c in consts],
        out_specs=[row(w) for w in widths],
        out_shape=[jax.ShapeDtypeStruct((t, w), BF16) for w in widths],
        compiler_params=_cparams(("parallel",)),
        name="mla_prep",
    )(mla_c, cos, sin, *consts)


def _mla_attn_body(q_ref, k_ref, v_ref, o_ref, sa_ref, sb_ref, pa_ref, pb_ref, ala_ref, alb_ref, m_ref, acc_ref):
    qi = pl.program_id(2)
    tq, tk, rc = MLA_TQ, MLA_TK, MLA_ROWS
    m_ref[...] = jnp.full(m_ref.shape, -jnp.inf, F32)
    acc_ref[...] = jnp.zeros(acc_ref.shape, F32)
    pb_ref[...] = jnp.zeros(pb_ref.shape, BF16)
    alb_ref[...] = jnp.ones(alb_ref.shape, F32)

    def scores(blk, s_ref, row0=0):
        start = pl.multiple_of(blk * tk, tk)
        for h in range(2):
            hl = slice(h * LANES, (h + 1) * LANES)
            s_ref[h, row0:, :] = _dot_nt(q_ref[0, row0:, hl], k_ref[0, pl.ds(start, tk), hl])

    def softmax(s_ref, p_ref, al_ref, diag_block=None, row0=0):
        for h in range(2):
            for r in range(row0, tq, rc):
                s = s_ref[h, r:r + rc, :]
                if diag_block is not None and diag_block * tk + tk - 1 > r:
                    qpos = lax.broadcasted_iota(jnp.int32, (rc, tk), 0) + r
                    kpos = lax.broadcasted_iota(jnp.int32, (rc, tk), 1) + diag_block * tk
                    s = jnp.where(kpos <= qpos, s, -jnp.inf)
                m_old = m_ref[h, r:r + rc, :]
                m_new = jnp.maximum(m_old, jnp.max(s, axis=-1, keepdims=True))
                al_ref[h, r:r + rc, :] = jnp.exp2(m_old - m_new)
                m_ref[h, r:r + rc, :] = m_new
                for c in range(0, tk, LANES):
                    p_ref[h, r:r + rc, c:c + LANES] = jnp.exp2(s[:, c:c + LANES] - m_new).astype(BF16)

    def weighted_values(blk, p_ref, al_ref, row0=0):
        start = pl.multiple_of(blk * tk, tk)
        for h in range(2):
            hl = slice(h * LANES, (h + 1) * LANES)
            acc_ref[h, row0:, :] = (al_ref[h, row0:, :] * acc_ref[h, row0:, :]
                                    + _dot(p_ref[h, row0:, :], v_ref[0, pl.ds(start, tk), hl]))

    scores(0, sa_ref)

    def block_pair(j, carry):
        weighted_values(jnp.maximum(2 * j - 1, 0), pb_ref, alb_ref)
        scores(2 * j + 1, sb_ref)
        softmax(sa_ref, pa_ref, ala_ref)
        weighted_values(2 * j, pa_ref, ala_ref)
        scores(2 * j + 2, sa_ref)
        softmax(sb_ref, pb_ref, alb_ref)
        return carry

    lax.fori_loop(0, qi, block_pair, 0)
    weighted_values(jnp.maximum(2 * qi - 1, 0), pb_ref, alb_ref)
    scores(2 * qi + 1, sb_ref, row0=tk)
    softmax(sa_ref, pa_ref, ala_ref, diag_block=0)
    weighted_values(2 * qi, pa_ref, ala_ref)
    softmax(sb_ref, pb_ref, alb_ref, diag_block=1, row0=tk)
    weighted_values(2 * qi + 1, pb_ref, alb_ref, row0=tk)

    lane = lax.broadcasted_iota(jnp.int32, (1, LANES), 1)
    a0, a1 = acc_ref[0], acc_ref[1]
    o0 = a0 * pltpu.roll(1.0 / a0, MLA_V, 1)
    o1 = pltpu.roll(a1, MLA_V, 1) * (1.0 / a1)
    o_ref[0] = jnp.where(lane < MLA_V, o0, o1).astype(o_ref.dtype)


def _mla_attention(q, k, v):
    b, s, _ = q.shape
    tq = MLA_TQ
    assert s % tq == 0
    npair = MLA_HEADS // 2
    return pl.pallas_call(
        _mla_attn_body,
        grid=(b, npair, s // tq),
        in_specs=[pl.BlockSpec((1, tq, 2 * LANES), lambda bi, hp, qi: (bi, qi, hp)),
                  pl.BlockSpec((1, s, 2 * LANES), lambda bi, hp, qi: (bi, 0, hp)),
                  pl.BlockSpec((1, s, 2 * LANES), lambda bi, hp, qi: (bi, 0, hp))],
        out_specs=pl.BlockSpec((1, tq, LANES), lambda bi, hp, qi: (bi, qi, hp)),
        out_shape=jax.ShapeDtypeStruct((b, s, MLA_WIDTH), BF16),
        scratch_shapes=[pltpu.VMEM((2, tq, MLA_TK), F32), pltpu.VMEM((2, tq, MLA_TK), F32),
                        pltpu.VMEM((2, tq, MLA_TK), BF16), pltpu.VMEM((2, tq, MLA_TK), BF16),
                        pltpu.VMEM((2, tq, LANES), F32), pltpu.VMEM((2, tq, LANES), F32),
                        pltpu.VMEM((2, tq, LANES), F32), pltpu.VMEM((2, tq, LANES), F32)],
        compiler_params=_cparams(("parallel", "parallel", "arbitrary")),
        name="mla_attention",
    )(q, k, v)


def _seg(z, name):
    off = 0
    for n, w in SEGMENTS:
        if n == name:
            return z[..., off:off + w]
        off += w
    raise KeyError(name)


def _pack_inproj(w, b):
    wb = jnp.concatenate([w, b[None, :]], axis=0)
    z = lambda n: jnp.zeros((wb.shape[0], n), wb.dtype)
    kr = jnp.concatenate([z(MLA_NOPE), _seg(wb, 'mla_kr'), z(LANES - MLA_QK)], axis=1)
    gif = jnp.concatenate([_seg(wb, 'ml_i'), _seg(wb, 'ml_f'), z(LANES - 2 * ML_HEADS)], axis=1)
    packed = jnp.concatenate(
        [_seg(wb, 'gates'), _seg(wb, 'ml_q'), _seg(wb, 'ml_k'), _seg(wb, 'ml_v'), _seg(wb, 'ml_o'),
         _seg(wb, 'sb_q'), _seg(wb, 'sb_k'), _seg(wb, 'sb_v'), _seg(wb, 'mla_cq'), _seg(wb, 'mla_ckv'),
         kr, gif], axis=1)
    return packed[:-1].astype(BF16), packed[-1:]


def _pad_heads(w, width, lo, hi):
    r = w.shape[0]
    wh = w.reshape(r, MLA_HEADS, width)[:, :, lo:hi]
    return jnp.pad(wh, ((0, 0), (0, 0), (0, LANES - (hi - lo)))).reshape(r, MLA_HEADS * LANES)


def kernel(x, positions, ffn1_norm, ffn1_wi, ffn1_wo, mix_norm, w_in, b_in, ml_conv_w, ml_conv_b,
           ml_out_norm, mla_q_norm, mla_kv_norm, mla_wq_up, mla_wkv_up, mla_q_gain, mla_k_gain,
           w_up_sb, w_up_ml, w_up_mla, w_out, ffn2_norm, ffn2_wi, ffn2_wo):
    b, s, d = x.shape
    t = b * s
    depth = w_in.shape[0]
    cos, sin = _rope_tables(positions)
    xt = x.reshape(t, d)
    row = lambda a: a[None, :]
    pad_gain = lambda g: jnp.pad(g, (0, LANES - MLA_QK))[None, :]
    for l in range(depth):
        xt = _ffn(xt, row(ffn1_norm[l]), ffn1_wi[l].astype(BF16), ffn1_wo[l].astype(BF16))

        w_packed, b_packed = _pack_inproj(w_in[l], b_in[l])
        gates, mlqk, mlvo, sbqkv, mla_c, gif = _inproj(xt, row(mix_norm[l]), w_packed, b_packed)

        ysb = _sb_attention(sbqkv.reshape(b, s, W_SB))

        yml = _mlstm(mlqk.reshape(b, s, W_MLQK), mlvo.reshape(b, s, W_MLVO), gif.reshape(b, s, W_IF),
                     ml_conv_w[l], row(ml_conv_b[l]), row(ml_out_norm[l]))

        wkv = mla_wkv_up[l]
        kvw = MLA_NOPE + MLA_V
        q_scale = MLA_QK ** -0.5 * np.log2(np.e)
        qh, kh, vh = _mla_prep(
            mla_c, cos, sin, row(mla_q_norm[l]), row(mla_kv_norm[l]),
            _pad_heads(mla_wq_up[l], MLA_QK, 0, MLA_QK).astype(BF16),
            _pad_heads(wkv, kvw, 0, MLA_NOPE).astype(BF16), _pad_heads(wkv, kvw, MLA_NOPE, kvw).astype(BF16),
            pad_gain(mla_q_gain[l] * q_scale), pad_gain(mla_k_gain[l]))
        ymla = _mla_attention(qh.reshape(b, s, -1), kh.reshape(b, s, -1), vh.reshape(b, s, -1))

        xt = _merge_ffn(xt, gates, ysb.reshape(t, -1), yml.reshape(t, -1), ymla.reshape(t, -1),
                        w_up_sb[l].astype(BF16), w_up_ml[l].astype(BF16), w_up_mla[l].astype(BF16),
                        w_out[l].astype(BF16), row(ffn2_norm[l]), ffn2_wi[l].astype(BF16), ffn2_wo[l].astype(BF16))
    return xt.reshape(b, s, d)
```

```python
import functools

import numpy as np
import jax
import jax.numpy as jnp
from jax import lax
from jax.experimental import pallas as pl
from jax.experimental.pallas import tpu as pltpu

F32 = jnp.float32
BF16 = jnp.bfloat16

EPS = 1e-6
LANES = 128
V7X_VMEM_LIMIT = 56 * 1024 * 1024

D_MODEL = 1024
FFN_DIM = 1408
SB_HEADS, SB_HEAD_DIM = 4, 64
SB_WIDTH = SB_HEADS * SB_HEAD_DIM
ML_HEADS, ML_HEAD_DIM = 4, 128
ML_WIDTH = ML_HEADS * ML_HEAD_DIM
ML_CHUNK = 128
ML_CONV = 4
MLA_HEADS, MLA_NOPE, MLA_ROPE, MLA_V = 4, 64, 32, 64
MLA_QK = MLA_NOPE + MLA_ROPE
MLA_WIDTH = MLA_HEADS * MLA_V
MLA_Q_RANK, MLA_KV_RANK = 256, 128
ROPE_THETA = 10000.0
N_BRANCH = 3

SEGMENTS = (
    ('sb_q', SB_WIDTH), ('sb_k', SB_WIDTH), ('sb_v', SB_WIDTH),
    ('ml_q', ML_WIDTH), ('ml_k', ML_WIDTH), ('ml_v', ML_WIDTH), ('ml_o', ML_WIDTH),
    ('ml_i', ML_HEADS), ('ml_f', ML_HEADS),
    ('mla_cq', MLA_Q_RANK), ('mla_ckv', MLA_KV_RANK), ('mla_kr', MLA_ROPE),
    ('gates', N_BRANCH * D_MODEL),
)

W_GATES = N_BRANCH * D_MODEL
W_MLQK = 2 * ML_WIDTH
W_MLVO = 2 * ML_WIDTH
W_SB = 3 * SB_WIDTH
W_MLA = MLA_Q_RANK + MLA_KV_RANK + LANES
W_IF = LANES
OUT_WIDTHS = (W_GATES, W_MLQK, W_MLVO, W_SB, W_MLA, W_IF)
N_PACKED = sum(OUT_WIDTHS)

TM_DENSE = 512
TQ_ATTN = 256
ML_TILE = 256
MLA_TQ = 1024
MLA_TK = 512
MLA_ROWS = 32
SB_EXP_UNDERFLOW = -120.0


def _cparams(sem):
    return pltpu.CompilerParams(dimension_semantics=sem, vmem_limit_bytes=V7X_VMEM_LIMIT)


def _const_spec(shape):
    nd = len(shape)
    return pl.BlockSpec(shape, lambda *_: (0,) * nd, pipeline_mode=pl.Buffered(1))


def _rms(x, gain):
    return x * lax.rsqrt(jnp.mean(x * x, axis=-1, keepdims=True) + EPS) * gain


def _dot(a, b):
    return jnp.dot(a, b, preferred_element_type=F32)


def _dot_nt(a, b):
    return lax.dot_general(a, b, (((1,), (1,)), ((), ())), preferred_element_type=F32)


def _dot_tn(a, b):
    return lax.dot_general(a, b, (((0,), (0,)), ((), ())), preferred_element_type=F32)


def _swiglu_residual(x, gain, wi_ref, wo_ref):
    u = _rms(x, gain).astype(BF16)
    h = _dot(u, wi_ref[...])
    a = h[:, :FFN_DIM]
    g = h[:, FFN_DIM:]
    act = (a * jax.nn.sigmoid(a) * g).astype(BF16)
    return x + 0.5 * _dot(act, wo_ref[...])


def _ffn_inproj_body(x_ref, g1_ref, wi_ref, wo_ref, gm_ref, w_ref, b_ref, x_out, *out_refs):
    x = _swiglu_residual(x_ref[...], g1_ref[...], wi_ref, wo_ref)
    x_out[...] = x
    u = _rms(x, gm_ref[...]).astype(BF16)
    off = 0
    for o_ref, width in zip(out_refs, OUT_WIDTHS):
        step = min(width, 1024)
        for c in range(0, width, step):
            z = _dot(u, w_ref[:, off + c:off + c + step]) + b_ref[:, off + c:off + c + step]
            o_ref[:, c:c + step] = z.astype(o_ref.dtype)
        off += width


def _ffn_inproj(x, g1, wi, wo, gm, w, b):
    t, d = x.shape
    tm = min(TM_DENSE, t)
    dtypes = (BF16, BF16, BF16, BF16, BF16, F32)
    consts = (g1, wi, wo, gm, w, b)
    return pl.pallas_call(
        _ffn_inproj_body,
        grid=(t // tm,),
        in_specs=[pl.BlockSpec((tm, d), lambda i: (i, 0))] + [_const_spec(c.shape) for c in consts],
        out_specs=[pl.BlockSpec((tm, d), lambda i: (i, 0))]
        + [pl.BlockSpec((tm, wd), lambda i: (i, 0)) for wd in OUT_WIDTHS],
        out_shape=[jax.ShapeDtypeStruct((t, d), F32)]
        + [jax.ShapeDtypeStruct((t, wd), dt) for wd, dt in zip(OUT_WIDTHS, dtypes)],
        compiler_params=_cparams(("parallel",)),
        name="ffn_inproj",
    )(x, *consts)


def _merge_body(x_ref, gt_ref, ysb_ref, yml_ref, ymla_ref, wsb_ref, wml_ref, wmla_ref, wout_ref,
                g2_ref, wi_ref, wo_ref, o_ref):
    d = D_MODEL
    merged = jax.nn.sigmoid(gt_ref[:, 0:d].astype(F32)) * _dot(ysb_ref[...], wsb_ref[...])
    merged += jax.nn.sigmoid(gt_ref[:, d:2 * d].astype(F32)) * _dot(yml_ref[...], wml_ref[...])
    merged += jax.nn.sigmoid(gt_ref[:, 2 * d:3 * d].astype(F32)) * _dot(ymla_ref[...], wmla_ref[...])
    x = x_ref[...] + _dot(merged.astype(BF16), wout_ref[...])
    o_ref[...] = _swiglu_residual(x, g2_ref[...], wi_ref, wo_ref)


def _merge_ffn(x, gates, ysb, yml, ymla, wsb, wml, wmla, wout, g2, wi, wo):
    t, d = x.shape
    tm = min(TM_DENSE, t)
    row = lambda a: pl.BlockSpec((tm, a.shape[1]), lambda i: (i, 0))
    consts = (wsb, wml, wmla, wout, g2, wi, wo)
    return pl.pallas_call(
        _merge_body,
        grid=(t // tm,),
        in_specs=[row(x), row(gates), row(ysb), row(yml), row(ymla)] + [_const_spec(c.shape) for c in consts],
        out_specs=pl.BlockSpec((tm, d), lambda i: (i, 0)),
        out_shape=jax.ShapeDtypeStruct((t, d), F32),
        compiler_params=_cparams(("parallel",)),
        name="merge_ffn",
    )(x, gates, ysb, yml, ymla, *consts)


def _sb_body(q_ref, k_ref, v_ref, o_ref, qm_ref, *, tq):
    qi = pl.program_id(1)
    lane = lax.broadcasted_iota(jnp.int32, (1, LANES), 1)
    qpos = lax.broadcasted_iota(jnp.int32, (tq, tq), 0)
    kpos = lax.broadcasted_iota(jnp.int32, (tq, tq), 1)
    tri = jnp.where(qpos > kpos, 1.0, 0.0).astype(BF16)
    strict = kpos < qpos

    scale = SB_HEAD_DIM ** -0.5
    for h in range(SB_HEADS):
        qp = q_ref[0, :, (h // 2) * LANES:(h // 2 + 1) * LANES]
        lo = (h % 2) * SB_HEAD_DIM
        qm_ref[h] = jnp.where((lane >= lo) & (lane < lo + SB_HEAD_DIM), qp, 0) * scale

    def sweep(kb, carry, masked):
        start = pl.multiple_of(kb * tq, tq)
        new = []
        for h in range(SB_HEADS):
            pair = slice((h // 2) * LANES, (h // 2 + 1) * LANES)
            run, acc = carry[h]
            s = _dot_nt(qm_ref[h], k_ref[0, pl.ds(start, tq), pair])
            ls = jnp.minimum(s, 0.0) - jnp.log(1.0 + jnp.exp(-jnp.abs(s)))
            lk = ls - s
            if masked:
                lk = jnp.where(strict, lk, 0.0)
            w = jnp.exp(ls + _dot(lk.astype(BF16), tri) + run)
            if masked:
                w = jnp.where(strict, w, 0.0)
            acc = acc + _dot(w.astype(BF16), v_ref[0, pl.ds(start, tq), pair])
            run = run + jnp.sum(lk, axis=-1, keepdims=True)
            new.append((run, acc))
        return tuple(new)

    def alive(carry):
        top = functools.reduce(jnp.maximum, [run for run, _ in carry])
        return (jnp.max(top) >= SB_EXP_UNDERFLOW).astype(jnp.int32)

    def body(state):
        kb, _, carry = state
        carry = sweep(kb, carry, False)
        return kb - 1, alive(carry), carry

    init = tuple((jnp.zeros((tq, 1), F32), jnp.zeros((tq, LANES), F32)) for _ in range(SB_HEADS))
    carry = sweep(qi, init, True)
    _, _, carry = lax.while_loop(lambda st: (st[0] >= 0) & (st[1] > 0), body, (qi - 1, alive(carry), carry))
    for p in range(SB_HEADS // 2):
        o_ref[0, :, p * LANES:(p + 1) * LANES] = jnp.where(
            lane < SB_HEAD_DIM, carry[2 * p][1], carry[2 * p + 1][1]).astype(o_ref.dtype)


def _sb_attention(sbqkv):
    b, s, _ = sbqkv.shape
    tq = min(TQ_ATTN, s)
    return pl.pallas_call(
        functools.partial(_sb_body, tq=tq),
        grid=(b, s // tq),
        in_specs=[pl.BlockSpec((1, tq, SB_WIDTH), lambda bi, qi: (bi, qi, 0)),
                  pl.BlockSpec((1, s, SB_WIDTH), lambda bi, qi: (bi, 0, 1)),
                  pl.BlockSpec((1, s, SB_WIDTH), lambda bi, qi: (bi, 0, 2))],
        out_specs=pl.BlockSpec((1, tq, SB_WIDTH), lambda bi, qi: (bi, qi, 0)),
        out_shape=jax.ShapeDtypeStruct((b, s, SB_WIDTH), BF16),
        scratch_shapes=[pltpu.VMEM((SB_HEADS, tq, LANES), BF16)],
        compiler_params=_cparams(("parallel", "arbitrary")),
        name="sb_attention",
    )(sbqkv, sbqkv, sbqkv)


def _split3(x):
    p1 = x.astype(BF16)
    r1 = x - p1.astype(F32)
    p2 = r1.astype(BF16)
    p3 = (r1 - p2.astype(F32)).astype(BF16)
    return p1, p2, p3


def _mlstm_body(qk_ref, vo_ref, if_ref, sh_ref, cw_ref, cb_ref, og_ref, o_ref, ext_ref, c_ref, n_ref, m_ref):
    L = ML_CHUNK

    @pl.when(pl.program_id(1) == 0)
    def _():
        ext_ref[0:L, :] = jnp.zeros((L, 2 * ML_WIDTH), BF16)
        c_ref[...] = jnp.zeros(c_ref.shape, F32)
        n_ref[...] = jnp.zeros(n_ref.shape, F32)
        m_ref[...] = jnp.zeros(m_ref.shape, F32)

    ext_ref[L:, :] = qk_ref[0]
    for sub in range(ML_TILE // L):
        _mlstm_chunk(sub * L, vo_ref, if_ref, sh_ref, cw_ref, cb_ref, og_ref, o_ref, ext_ref, c_ref, n_ref, m_ref)
    ext_ref[0:L, :] = ext_ref[ML_TILE:, :]


def _mlstm_chunk(r0, vo_ref, if_ref, sh_ref, cw_ref, cb_ref, og_ref, o_ref, ext_ref, c_ref, n_ref, m_ref):
    L = ML_CHUNK
    hd = ML_HEAD_DIM
    rows = slice(r0, r0 + L)
    win = ext_ref[r0:r0 + 2 * L, :]
    shifted = _dot(sh_ref[...], win)
    conv = cb_ref[...] + cw_ref[ML_CONV - 1:ML_CONV, :] * win[L:, :].astype(F32)
    for j in range(1, ML_CONV):
        conv += cw_ref[ML_CONV - 1 - j:ML_CONV - j, :] * shifted[(j - 1) * L:j * L, :]
    qk = conv * jax.nn.sigmoid(conv)

    gl = lax.broadcasted_iota(jnp.int32, (1, LANES), 1)
    gin = if_ref[0, rows, :]
    gates = jnp.where(gl < ML_HEADS, gin, jax.nn.log_sigmoid(gin))
    rr = lax.broadcasted_iota(jnp.int32, (L, L), 0)
    cc = lax.broadcasted_iota(jnp.int32, (L, L), 1)
    causal = rr >= cc
    tril = jnp.where(causal, 1.0, 0.0).astype(BF16)
    cum = sum(_dot(tril, p) for p in _split3(gates))
    gates_t = gates.T
    cum_t = cum.T

    for h in range(ML_HEADS):
        qf = qk[:, h * hd:(h + 1) * hd]
        qh = qf.astype(BF16)
        kf = qk[:, ML_WIDTH + h * hd:ML_WIDTH + (h + 1) * hd] * (hd ** -0.5)
        kh = kf.astype(BF16)
        vh = vo_ref[0, rows, h * hd:(h + 1) * hd]
        c_prev = c_ref[h]
        n_prev = n_ref[h:h + 1, :]
        m_prev = m_ref[h:h + 1, 0:1]

        cf_col = cum[:, ML_HEADS + h:ML_HEADS + h + 1]
        ig_col = gates[:, h:h + 1]
        cf_row = cum_t[ML_HEADS + h:ML_HEADS + h + 1, :]
        ig_row = gates_t[h:h + 1, :]

        log_intra = jnp.where(causal, cf_col - cf_row + ig_row, -jnp.inf)
        log_inter = cf_col + m_prev
        m = jnp.maximum(log_inter, jnp.max(log_intra, axis=-1, keepdims=True))
        w_intra = jnp.exp(log_intra - m)
        w_inter = jnp.exp(log_inter - m)
        scores = _dot_nt(qh, kh) * w_intra
        num = _dot(scores.astype(BF16), vh) + w_inter * _dot(qh, c_prev.astype(BF16))
        qn = jnp.sum(qf * n_prev, axis=-1, keepdims=True)
        den = jnp.sum(scores, axis=-1, keepdims=True) + w_inter * qn
        hout = num / jnp.maximum(jnp.abs(den), jnp.exp(-m))

        f_total = cf_col[L - 1:L, :]
        log_to_end = f_total - cf_col + ig_col
        m_new = jnp.maximum(f_total + m_prev, jnp.max(log_to_end, axis=0, keepdims=True))
        decay = jnp.exp(f_total + m_prev - m_new)
        w_end = jnp.exp(log_to_end - m_new)
        kw = kf * w_end
        c_ref[h] = decay * c_prev + _dot_tn(kw.astype(BF16), vh)
        n_ref[h:h + 1, :] = decay * n_prev + jnp.sum(kw, axis=0, keepdims=True)
        m_ref[h:h + 1, :] = jnp.broadcast_to(m_new, (1, LANES))

        hn = _rms(hout, og_ref[:, h * hd:(h + 1) * hd])
        o_pre = vo_ref[0, rows, ML_WIDTH + h * hd:ML_WIDTH + (h + 1) * hd].astype(F32)
        o_ref[0, rows, h * hd:(h + 1) * hd] = (jax.nn.sigmoid(o_pre) * hn).astype(o_ref.dtype)


def _conv_shift_matrix():
    L = ML_CHUNK
    sh = np.zeros(((ML_CONV - 1) * L, 2 * L), np.float32)
    for j in range(1, ML_CONV):
        sh[(j - 1) * L + np.arange(L), L + np.arange(L) - j] = 1.0
    return jnp.asarray(sh, BF16)


def _mlstm(mlqk, mlvo, gif, conv_w, conv_b, out_gain):
    b, s, _ = mlqk.shape
    tl = ML_TILE
    assert s % tl == 0
    shift = _conv_shift_matrix()
    blk = lambda w: pl.BlockSpec((1, tl, w), lambda bi, ci: (bi, ci, 0))
    return pl.pallas_call(
        _mlstm_body,
        grid=(b, s // tl),
        in_specs=[blk(W_MLQK), blk(W_MLVO), blk(W_IF), _const_spec(shift.shape),
                  _const_spec(conv_w.shape), _const_spec(conv_b.shape), _const_spec(out_gain.shape)],
        out_specs=blk(ML_WIDTH),
        out_shape=jax.ShapeDtypeStruct((b, s, ML_WIDTH), BF16),
        scratch_shapes=[pltpu.VMEM((ML_CHUNK + tl, W_MLQK), BF16),
                        pltpu.VMEM((ML_HEADS, ML_HEAD_DIM, ML_HEAD_DIM), F32),
                        pltpu.VMEM((8, ML_HEAD_DIM), F32),
                        pltpu.VMEM((8, LANES), F32)],
        compiler_params=_cparams(("parallel", "arbitrary")),
        name="mlstm",
    )(mlqk, mlvo, gif, shift, conv_w, conv_b, out_gain)


def _rope_table_body(pos_ref, invf_ref, sign_ref, cos_ref, sin_ref):
    ang = pos_ref[...].astype(F32) * invf_ref[...]
    cos_ref[...] = jnp.cos(ang)
    sin_ref[...] = jnp.sin(ang) * sign_ref[...]


_ROPE_HALF = MLA_ROPE // 2
_HEAD_SRC = np.concatenate([
    np.arange(MLA_NOPE, MLA_NOPE + _ROPE_HALF), np.arange(0, MLA_NOPE - _ROPE_HALF),
    np.arange(MLA_NOPE + _ROPE_HALF, MLA_QK), np.arange(MLA_NOPE - _ROPE_HALF, MLA_NOPE)])


def _head_lanes(a):
    a = a[..., _HEAD_SRC]
    return jnp.pad(a, [(0, 0)] * (a.ndim - 1) + [(0, LANES - MLA_QK)])


def _rope_tables(positions):
    t = positions.size
    half = _ROPE_HALF
    inv_freq = jnp.power(ROPE_THETA, -jnp.arange(half, dtype=F32) / half)
    zeros = jnp.zeros((half,), F32)
    nope = jnp.zeros((MLA_NOPE,), F32)
    invf = _head_lanes(jnp.concatenate([nope, inv_freq, inv_freq]))[None, :]
    sign = _head_lanes(jnp.concatenate([nope, zeros - 1.0, zeros + 1.0]))[None, :]
    tm = min(TM_DENSE, t)
    return pl.pallas_call(
        _rope_table_body,
        grid=(t // tm,),
        in_specs=[pl.BlockSpec((tm, 1), lambda i: (i, 0)), _const_spec(invf.shape), _const_spec(sign.shape)],
        out_specs=[pl.BlockSpec((tm, LANES), lambda i: (i, 0))] * 2,
        out_shape=[jax.ShapeDtypeStruct((t, LANES), F32)] * 2,
        compiler_params=_cparams(("parallel",)),
        name="rope_tables",
    )(positions.reshape(t, 1), invf, sign)


def _mla_prep_body(c_ref, cos_ref, sin_ref, qn_ref, kvn_ref, wq_ref, wk_ref, wv_ref, qg_ref, kg_ref,
                   q_out, k_out, v_out):
    c = c_ref[...].astype(F32)
    cq = _rms(c[:, :MLA_Q_RANK], qn_ref[...]).astype(BF16)
    ckv = _rms(c[:, MLA_Q_RANK:MLA_Q_RANK + MLA_KV_RANK], kvn_ref[...]).astype(BF16)
    k_rope = c[:, MLA_Q_RANK + MLA_KV_RANK:]
    q = _dot(cq, wq_ref[...])
    k = _dot(ckv, wk_ref[...])
    vlane = lax.broadcasted_iota(jnp.int32, (1, MLA_HEADS * LANES), 1)
    ones_half = jnp.where(vlane % LANES >= MLA_V, 1.0, 0.0)
    v_out[...] = (_dot(ckv, wv_ref[...]) + ones_half).astype(v_out.dtype)
    cos = cos_ref[...]
    sin = sin_ref[...]

    def norm_rope(x, gain):
        y = x * lax.rsqrt(jnp.sum(x * x, axis=-1, keepdims=True) * (1.0 / MLA_QK) + EPS) * gain
        return y * cos + pltpu.roll(y, LANES // 2, 1) * sin

    for h in range(MLA_HEADS):
        sl = slice(h * LANES, (h + 1) * LANES)
        q_out[:, sl] = norm_rope(q[:, sl], qg_ref[...]).astype(q_out.dtype)
        k_out[:, sl] = norm_rope(k[:, sl] + k_rope, kg_ref[...]).astype(k_out.dtype)


def _mla_prep(mla_c, cos, sin, qn, kvn, wq, wk, wv, qg, kg):
    t = mla_c.shape[0]
    tm = min(TM_DENSE, t)
    row = lambda w: pl.BlockSpec((tm, w), lambda i: (i, 0))
    consts = (qn, kvn, wq, wk, wv, qg, kg)
    widths = (MLA_HEADS * LANES,) * 3
    return pl.pallas_call(
        _mla_prep_body,
        grid=(t // tm,),
        in_specs=[row(W_MLA), row(LANES), row(LANES)] + [_const_spec(c.shape) for c in consts],
        out_specs=[row(w) for w in widths],
        out_shape=[jax.ShapeDtypeStruct((t, w), BF16) for w in widths],
        compiler_params=_cparams(("parallel",)),
        name="mla_prep",
    )(mla_c, cos, sin, *consts)


def _mla_attn_body(q_ref, k_ref, v_ref, o_ref, sa_ref, sb_ref, pa_ref, pb_ref, ala_ref, alb_ref, m_ref, acc_ref):
    qi = pl.program_id(2)
    tq, tk, rc = MLA_TQ, MLA_TK, MLA_ROWS
    m_ref[...] = jnp.full(m_ref.shape, -jnp.inf, F32)
    acc_ref[...] = jnp.zeros(acc_ref.shape, F32)
    pb_ref[...] = jnp.zeros(pb_ref.shape, BF16)
    alb_ref[...] = jnp.ones(alb_ref.shape, F32)

    def scores(blk, s_ref, row0=0):
        start = pl.multiple_of(blk * tk, tk)
        for h in range(2):
            hl = slice(h * LANES, (h + 1) * LANES)
            s_ref[h, row0:, :] = _dot_nt(q_ref[0, row0:, hl], k_ref[0, pl.ds(start, tk), hl])

    def softmax(s_ref, p_ref, al_ref, diag_block=None, row0=0):
        for h in range(2):
            for r in range(row0, tq, rc):
                s = s_ref[h, r:r + rc, :]
                if diag_block is not None and diag_block * tk + tk - 1 > r:
                    qpos = lax.broadcasted_iota(jnp.int32, (rc, tk), 0) + r
                    kpos = lax.broadcasted_iota(jnp.int32, (rc, tk), 1) + diag_block * tk
                    s = jnp.where(kpos <= qpos, s, -jnp.inf)
                m_old = m_ref[h, r:r + rc, :]
                m_new = jnp.maximum(m_old, jnp.max(s, axis=-1, keepdims=True))
                al_ref[h, r:r + rc, :] = jnp.exp2(m_old - m_new)
                m_ref[h, r:r + rc, :] = m_new
                for c in range(0, tk, LANES):
                    p_ref[h, r:r + rc, c:c + LANES] = jnp.exp2(s[:, c:c + LANES] - m_new).astype(BF16)

    def weighted_values(blk, p_ref, al_ref, row0=0):
        start = pl.multiple_of(blk * tk, tk)
        for h in range(2):
            hl = slice(h * LANES, (h + 1) * LANES)
            acc_ref[h, row0:, :] = (al_ref[h, row0:, :] * acc_ref[h, row0:, :]
                                    + _dot(p_ref[h, row0:, :], v_ref[0, pl.ds(start, tk), hl]))

    scores(0, sa_ref)

    def block_pair(j, carry):
        weighted_values(jnp.maximum(2 * j - 1, 0), pb_ref, alb_ref)
        scores(2 * j + 1, sb_ref)
        softmax(sa_ref, pa_ref, ala_ref)
        weighted_values(2 * j, pa_ref, ala_ref)
        scores(2 * j + 2, sa_ref)
        softmax(sb_ref, pb_ref, alb_ref)
        return carry

    lax.fori_loop(0, qi, block_pair, 0)
    weighted_values(jnp.maximum(2 * qi - 1, 0), pb_ref, alb_ref)
    scores(2 * qi + 1, sb_ref, row0=tk)
    softmax(sa_ref, pa_ref, ala_ref, diag_block=0)
    weighted_values(2 * qi, pa_ref, ala_ref)
    softmax(sb_ref, pb_ref, alb_ref, diag_block=1, row0=tk)
    weighted_values(2 * qi + 1, pb_ref, alb_ref, row0=tk)

    lane = lax.broadcasted_iota(jnp.int32, (1, LANES), 1)
    a0, a1 = acc_ref[0], acc_ref[1]
    o0 = a0 * pltpu.roll(1.0 / a0, MLA_V, 1)
    o1 = pltpu.roll(a1, MLA_V, 1) * (1.0 / a1)
    o_ref[0] = jnp.where(lane < MLA_V, o0, o1).astype(o_ref.dtype)


def _mla_attention(q, k, v):
    b, s, _ = q.shape
    tq = MLA_TQ
    assert s % tq == 0
    npair = MLA_HEADS // 2
    return pl.pallas_call(
        _mla_attn_body,
        grid=(b, npair, s // tq),
        in_specs=[pl.BlockSpec((1, tq, 2 * LANES), lambda bi, hp, qi: (bi, qi, hp)),
                  pl.BlockSpec((1, s, 2 * LANES), lambda bi, hp, qi: (bi, 0, hp)),
                  pl.BlockSpec((1, s, 2 * LANES), lambda bi, hp, qi: (bi, 0, hp))],
        out_specs=pl.BlockSpec((1, tq, LANES), lambda bi, hp, qi: (bi, qi, hp)),
        out_shape=jax.ShapeDtypeStruct((b, s, MLA_WIDTH), BF16),
        scratch_shapes=[pltpu.VMEM((2, tq, MLA_TK), F32), pltpu.VMEM((2, tq, MLA_TK), F32),
                        pltpu.VMEM((2, tq, MLA_TK), BF16), pltpu.VMEM((2, tq, MLA_TK), BF16),
                        pltpu.VMEM((2, tq, LANES), F32), pltpu.VMEM((2, tq, LANES), F32),
                        pltpu.VMEM((2, tq, LANES), F32), pltpu.VMEM((2, tq, LANES), F32)],
        compiler_params=_cparams(("parallel", "parallel", "arbitrary")),
        name="mla_attention",
    )(q, k, v)


def _seg(z, name):
    off = 0
    for n, w in SEGMENTS:
        if n == name:
            return z[..., off:off + w]
        off += w
    raise KeyError(name)


def _pack_inproj(w, b):
    wb = jnp.concatenate([w, b[None, :]], axis=0)
    z = lambda n: jnp.zeros((wb.shape[0], n), wb.dtype)
    kr = _head_lanes(jnp.concatenate([z(MLA_NOPE), _seg(wb, 'mla_kr')], axis=1))
    gif = jnp.concatenate([_seg(wb, 'ml_i'), _seg(wb, 'ml_f'), z(LANES - 2 * ML_HEADS)], axis=1)
    packed = jnp.concatenate(
        [_seg(wb, 'gates'), _seg(wb, 'ml_q'), _seg(wb, 'ml_k'), _seg(wb, 'ml_v'), _seg(wb, 'ml_o'),
         _seg(wb, 'sb_q'), _seg(wb, 'sb_k'), _seg(wb, 'sb_v'), _seg(wb, 'mla_cq'), _seg(wb, 'mla_ckv'),
         kr, gif], axis=1)
    return packed[:-1].astype(BF16), packed[-1:]


def _mla_up_weights(wq_up, wkv_up):
    hq = wq_up.reshape(MLA_Q_RANK, MLA_HEADS, MLA_QK)
    hkv = wkv_up.reshape(MLA_KV_RANK, MLA_HEADS, MLA_NOPE + MLA_V)
    k_nope = jnp.pad(hkv[:, :, :MLA_NOPE], ((0, 0), (0, 0), (0, MLA_ROPE)))
    v = jnp.pad(hkv[:, :, MLA_NOPE:], ((0, 0), (0, 0), (0, LANES - MLA_V)))
    flat = lambda a: a.reshape(a.shape[0], MLA_HEADS * LANES).astype(BF16)
    return flat(_head_lanes(hq)), flat(_head_lanes(k_nope)), flat(v)


def kernel(x, positions, ffn1_norm, ffn1_wi, ffn1_wo, mix_norm, w_in, b_in, ml_conv_w, ml_conv_b,
           ml_out_norm, mla_q_norm, mla_kv_norm, mla_wq_up, mla_wkv_up, mla_q_gain, mla_k_gain,
           w_up_sb, w_up_ml, w_up_mla, w_out, ffn2_norm, ffn2_wi, ffn2_wo):
    b, s, d = x.shape
    t = b * s
    depth = w_in.shape[0]
    cos, sin = _rope_tables(positions)
    xt = x.reshape(t, d)
    row = lambda a: a[None, :]
    for l in range(depth):
        w_packed, b_packed = _pack_inproj(w_in[l], b_in[l])
        xt, gates, mlqk, mlvo, sbqkv, mla_c, gif = _ffn_inproj(
            xt, row(ffn1_norm[l]), ffn1_wi[l].astype(BF16), ffn1_wo[l].astype(BF16),
            row(mix_norm[l]), w_packed, b_packed)

        ysb = _sb_attention(sbqkv.reshape(b, s, W_SB))

        yml = _mlstm(mlqk.reshape(b, s, W_MLQK), mlvo.reshape(b, s, W_MLVO), gif.reshape(b, s, W_IF),
                     ml_conv_w[l], row(ml_conv_b[l]), row(ml_out_norm[l]))

        q_scale = MLA_QK ** -0.5 * np.log2(np.e)
        qh, kh, vh = _mla_prep(
            mla_c, cos, sin, row(mla_q_norm[l]), row(mla_kv_norm[l]),
            *_mla_up_weights(mla_wq_up[l], mla_wkv_up[l]),
            row(_head_lanes(mla_q_gain[l] * q_scale)), row(_head_lanes(mla_k_gain[l])))
        ymla = _mla_attention(qh.reshape(b, s, -1), kh.reshape(b, s, -1), vh.reshape(b, s, -1))

        xt = _merge_ffn(xt, gates, ysb.reshape(t, -1), yml.reshape(t, -1), ymla.reshape(t, -1),
                        w_up_sb[l].astype(BF16), w_up_ml[l].astype(BF16), w_up_mla[l].astype(BF16),
                        w_out[l].astype(BF16), row(ffn2_norm[l]), ffn2_wi[l].astype(BF16), ffn2_wo[l].astype(BF16))
    return xt.reshape(b, s, d)
```

```python
import functools

import numpy as np
import jax
import jax.numpy as jnp
from jax import lax
from jax.experimental import pallas as pl
from jax.experimental.pallas import tpu as pltpu

F32 = jnp.float32
BF16 = jnp.bfloat16

EPS = 1e-6
LANES = 128
V7X_VMEM_LIMIT = 56 * 1024 * 1024

D_MODEL = 1024
FFN_DIM = 1408
SB_HEADS, SB_HEAD_DIM = 4, 64
SB_WIDTH = SB_HEADS * SB_HEAD_DIM
ML_HEADS, ML_HEAD_DIM = 4, 128
ML_WIDTH = ML_HEADS * ML_HEAD_DIM
ML_CHUNK = 128
ML_CONV = 4
MLA_HEADS, MLA_NOPE, MLA_ROPE, MLA_V = 4, 64, 32, 64
MLA_QK = MLA_NOPE + MLA_ROPE
MLA_WIDTH = MLA_HEADS * MLA_V
MLA_Q_RANK, MLA_KV_RANK = 256, 128
ROPE_THETA = 10000.0
N_BRANCH = 3

SEGMENTS = (
    ('sb_q', SB_WIDTH), ('sb_k', SB_WIDTH), ('sb_v', SB_WIDTH),
    ('ml_q', ML_WIDTH), ('ml_k', ML_WIDTH), ('ml_v', ML_WIDTH), ('ml_o', ML_WIDTH),
    ('ml_i', ML_HEADS), ('ml_f', ML_HEADS),
    ('mla_cq', MLA_Q_RANK), ('mla_ckv', MLA_KV_RANK), ('mla_kr', MLA_ROPE),
    ('gates', N_BRANCH * D_MODEL),
)

W_GATES = N_BRANCH * D_MODEL
W_MLQK = 2 * ML_WIDTH
W_MLVO = 2 * ML_WIDTH
W_SB = 3 * SB_WIDTH
W_MLA = MLA_Q_RANK + MLA_KV_RANK + LANES
W_IF = LANES
OUT_WIDTHS = (W_GATES, W_MLQK, W_MLVO, W_SB, W_MLA, W_IF)

TM_DENSE = 512
PROJ_CHUNK = 1024
SB_TQ = 256
ML_TILE = 512
MLA_TQ = 1024
MLA_TK = 512
MLA_ROWS = 32
SB_EXP_UNDERFLOW = -120.0


def _cparams(sem):
    return pltpu.CompilerParams(dimension_semantics=sem, vmem_limit_bytes=V7X_VMEM_LIMIT)


def _const_spec(shape):
    nd = len(shape)
    return pl.BlockSpec(shape, lambda *_: (0,) * nd, pipeline_mode=pl.Buffered(1))


def _rms(x, gain):
    return x * lax.rsqrt(jnp.mean(x * x, axis=-1, keepdims=True) + EPS) * gain


def _dot(a, b):
    return jnp.dot(a, b, preferred_element_type=F32)


def _dot_nt(a, b):
    return lax.dot_general(a, b, (((1,), (1,)), ((), ())), preferred_element_type=F32)


def _dot_tn(a, b):
    return lax.dot_general(a, b, (((0,), (0,)), ((), ())), preferred_element_type=F32)


def _swiglu_residual(x, gain, wi_ref, wo_ref):
    u = _rms(x, gain).astype(BF16)
    h = _dot(u, wi_ref[...])
    a = h[:, :FFN_DIM]
    g = h[:, FFN_DIM:]
    act = (a * jax.nn.sigmoid(a) * g).astype(BF16)
    return x + 0.5 * _dot(act, wo_ref[...])


def _ffn_inproj_body(x_ref, g1_ref, wi_ref, wo_ref, gm_ref, w_ref, b_ref, x_out, *out_refs):
    x = _swiglu_residual(x_ref[...], g1_ref[...], wi_ref, wo_ref)
    x_out[...] = x
    u = _rms(x, gm_ref[...]).astype(BF16)
    off = 0
    for o_ref, width in zip(out_refs, OUT_WIDTHS):
        step = min(width, PROJ_CHUNK)
        for c in range(0, width, step):
            z = _dot(u, w_ref[:, off + c:off + c + step]) + b_ref[:, off + c:off + c + step]
            o_ref[:, c:c + step] = z.astype(o_ref.dtype)
        off += width


def _ffn_inproj(x, g1, wi, wo, gm, w, b):
    t, d = x.shape
    tm = min(TM_DENSE, t)
    dtypes = (BF16, BF16, BF16, BF16, BF16, F32)
    consts = (g1, wi, wo, gm, w, b)
    return pl.pallas_call(
        _ffn_inproj_body,
        grid=(t // tm,),
        in_specs=[pl.BlockSpec((tm, d), lambda i: (i, 0))] + [_const_spec(c.shape) for c in consts],
        out_specs=[pl.BlockSpec((tm, d), lambda i: (i, 0))]
        + [pl.BlockSpec((tm, wd), lambda i: (i, 0)) for wd in OUT_WIDTHS],
        out_shape=[jax.ShapeDtypeStruct((t, d), F32)]
        + [jax.ShapeDtypeStruct((t, wd), dt) for wd, dt in zip(OUT_WIDTHS, dtypes)],
        compiler_params=_cparams(("parallel",)),
        name="ffn_inproj",
    )(x, *consts)


def _merge_body(x_ref, gt_ref, ysb_ref, yml_ref, ymla_ref, wsb_ref, wml_ref, wmla_ref, wout_ref,
                g2_ref, wi_ref, wo_ref, o_ref):
    d = D_MODEL
    merged = jax.nn.sigmoid(gt_ref[:, 0:d].astype(F32)) * _dot(ysb_ref[...], wsb_ref[...])
    merged += jax.nn.sigmoid(gt_ref[:, d:2 * d].astype(F32)) * _dot(yml_ref[...], wml_ref[...])
    merged += jax.nn.sigmoid(gt_ref[:, 2 * d:3 * d].astype(F32)) * _dot(ymla_ref[...], wmla_ref[...])
    x = x_ref[...] + _dot(merged.astype(BF16), wout_ref[...])
    o_ref[...] = _swiglu_residual(x, g2_ref[...], wi_ref, wo_ref)


def _merge_ffn(x, gates, ysb, yml, ymla, wsb, wml, wmla, wout, g2, wi, wo):
    t, d = x.shape
    tm = min(TM_DENSE, t)
    row = lambda a: pl.BlockSpec((tm, a.shape[1]), lambda i: (i, 0))
    consts = (wsb, wml, wmla, wout, g2, wi, wo)
    return pl.pallas_call(
        _merge_body,
        grid=(t // tm,),
        in_specs=[row(x), row(gates), row(ysb), row(yml), row(ymla)] + [_const_spec(c.shape) for c in consts],
        out_specs=pl.BlockSpec((tm, d), lambda i: (i, 0)),
        out_shape=jax.ShapeDtypeStruct((t, d), F32),
        compiler_params=_cparams(("parallel",)),
        name="merge_ffn",
    )(x, gates, ysb, yml, ymla, *consts)


def _sb_body(q_ref, k_ref, v_ref, o_ref, qm_ref, *, tq):
    qi = pl.program_id(1)
    lane = lax.broadcasted_iota(jnp.int32, (1, LANES), 1)
    qpos = lax.broadcasted_iota(jnp.int32, (tq, tq), 0)
    kpos = lax.broadcasted_iota(jnp.int32, (tq, tq), 1)
    tri = jnp.where(qpos > kpos, 1.0, 0.0).astype(BF16)
    strict = kpos < qpos

    scale = SB_HEAD_DIM ** -0.5
    for h in range(SB_HEADS):
        qp = q_ref[0, :, (h // 2) * LANES:(h // 2 + 1) * LANES]
        lo = (h % 2) * SB_HEAD_DIM
        qm_ref[h] = jnp.where((lane >= lo) & (lane < lo + SB_HEAD_DIM), qp, 0) * scale

    def sweep(kb, carry, masked):
        start = pl.multiple_of(kb * tq, tq)
        new = []
        for h in range(SB_HEADS):
            pair = slice((h // 2) * LANES, (h // 2 + 1) * LANES)
            run, acc = carry[h]
            s = _dot_nt(qm_ref[h], k_ref[0, pl.ds(start, tq), pair])
            ls = jnp.minimum(s, 0.0) - jnp.log(1.0 + jnp.exp(-jnp.abs(s)))
            lk = ls - s
            if masked:
                lk = jnp.where(strict, lk, 0.0)
            w = jnp.exp(ls + _dot(lk.astype(BF16), tri) + run)
            if masked:
                w = jnp.where(strict, w, 0.0)
            acc = acc + _dot(w.astype(BF16), v_ref[0, pl.ds(start, tq), pair])
            run = run + jnp.sum(lk, axis=-1, keepdims=True)
            new.append((run, acc))
        return tuple(new)

    def alive(carry):
        top = functools.reduce(jnp.maximum, [run for run, _ in carry])
        return (jnp.max(top) >= SB_EXP_UNDERFLOW).astype(jnp.int32)

    def body(state):
        kb, _, carry = state
        carry = sweep(kb, carry, False)
        return kb - 1, alive(carry), carry

    init = tuple((jnp.zeros((tq, 1), F32), jnp.zeros((tq, LANES), F32)) for _ in range(SB_HEADS))
    carry = sweep(qi, init, True)
    _, _, carry = lax.while_loop(lambda st: (st[0] >= 0) & (st[1] > 0), body, (qi - 1, alive(carry), carry))
    for p in range(SB_HEADS // 2):
        o_ref[0, :, p * LANES:(p + 1) * LANES] = jnp.where(
            lane < SB_HEAD_DIM, carry[2 * p][1], carry[2 * p + 1][1]).astype(o_ref.dtype)


def _sb_attention(sbqkv):
    b, s, _ = sbqkv.shape
    tq = min(SB_TQ, s)
    return pl.pallas_call(
        functools.partial(_sb_body, tq=tq),
        grid=(b, s // tq),
        in_specs=[pl.BlockSpec((1, tq, SB_WIDTH), lambda bi, qi: (bi, qi, 0)),
                  pl.BlockSpec((1, s, SB_WIDTH), lambda bi, qi: (bi, 0, 1)),
                  pl.BlockSpec((1, s, SB_WIDTH), lambda bi, qi: (bi, 0, 2))],
        out_specs=pl.BlockSpec((1, tq, SB_WIDTH), lambda bi, qi: (bi, qi, 0)),
        out_shape=jax.ShapeDtypeStruct((b, s, SB_WIDTH), BF16),
        scratch_shapes=[pltpu.VMEM((SB_HEADS, tq, LANES), BF16)],
        compiler_params=_cparams(("parallel", "arbitrary")),
        name="sb_attention",
    )(sbqkv, sbqkv, sbqkv)


def _split3(x):
    p1 = x.astype(BF16)
    r1 = x - p1.astype(F32)
    p2 = r1.astype(BF16)
    p3 = (r1 - p2.astype(F32)).astype(BF16)
    return p1, p2, p3


def _mlstm_body(qk_ref, vo_ref, if_ref, sh_ref, cw_ref, cb_ref, og_ref, o_ref, ext_ref, c_ref, n_ref, m_ref):
    L = ML_CHUNK

    @pl.when(pl.program_id(1) == 0)
    def _():
        ext_ref[0:L, :] = jnp.zeros((L, 2 * ML_WIDTH), BF16)
        c_ref[...] = jnp.zeros(c_ref.shape, F32)
        n_ref[...] = jnp.zeros(n_ref.shape, F32)
        m_ref[...] = jnp.zeros(m_ref.shape, F32)

    ext_ref[L:, :] = qk_ref[0]
    for sub in range(ML_TILE // L):
        _mlstm_chunk(sub * L, vo_ref, if_ref, sh_ref, cw_ref, cb_ref, og_ref, o_ref, ext_ref, c_ref, n_ref, m_ref)
    ext_ref[0:L, :] = ext_ref[ML_TILE:, :]


def _mlstm_chunk(r0, vo_ref, if_ref, sh_ref, cw_ref, cb_ref, og_ref, o_ref, ext_ref, c_ref, n_ref, m_ref):
    L = ML_CHUNK
    hd = ML_HEAD_DIM
    rows = slice(r0, r0 + L)
    win = ext_ref[r0:r0 + 2 * L, :]
    shifted = _dot(sh_ref[...], win)
    conv = cb_ref[...] + cw_ref[ML_CONV - 1:ML_CONV, :] * win[L:, :].astype(F32)
    for j in range(1, ML_CONV):
        conv += cw_ref[ML_CONV - 1 - j:ML_CONV - j, :] * shifted[(j - 1) * L:j * L, :]
    qk = conv * jax.nn.sigmoid(conv)

    gl = lax.broadcasted_iota(jnp.int32, (1, LANES), 1)
    gin = if_ref[0, rows, :]
    gates = jnp.where(gl < ML_HEADS, gin, jax.nn.log_sigmoid(gin))
    rr = lax.broadcasted_iota(jnp.int32, (L, L), 0)
    cc = lax.broadcasted_iota(jnp.int32, (L, L), 1)
    causal = rr >= cc
    tril = jnp.where(causal, 1.0, 0.0).astype(BF16)
    cum = sum(_dot(tril, p) for p in _split3(gates))
    gates_t = gates.T
    cum_t = cum.T

    for h in range(ML_HEADS):
        qf = qk[:, h * hd:(h + 1) * hd]
        qh = qf.astype(BF16)
        kf = qk[:, ML_WIDTH + h * hd:ML_WIDTH + (h + 1) * hd] * (hd ** -0.5)
        kh = kf.astype(BF16)
        vh = vo_ref[0, rows, h * hd:(h + 1) * hd]
        c_prev = c_ref[h]
        n_prev = n_ref[h:h + 1, :]
        m_prev = m_ref[h:h + 1, 0:1]

        cf_col = cum[:, ML_HEADS + h:ML_HEADS + h + 1]
        ig_col = gates[:, h:h + 1]
        cf_row = cum_t[ML_HEADS + h:ML_HEADS + h + 1, :]
        ig_row = gates_t[h:h + 1, :]

        log_intra = jnp.where(causal, cf_col - cf_row + ig_row, -jnp.inf)
        log_inter = cf_col + m_prev
        m = jnp.maximum(log_inter, jnp.max(log_intra, axis=-1, keepdims=True))
        w_intra = jnp.exp(log_intra - m)
        w_inter = jnp.exp(log_inter - m)
        scores = _dot_nt(qh, kh) * w_intra
        num = _dot(scores.astype(BF16), vh) + w_inter * _dot(qh, c_prev.astype(BF16))
        qn = jnp.sum(qf * n_prev, axis=-1, keepdims=True)
        den = jnp.sum(scores, axis=-1, keepdims=True) + w_inter * qn
        hout = num / jnp.maximum(jnp.abs(den), jnp.exp(-m))

        f_total = cf_col[L - 1:L, :]
        log_to_end = f_total - cf_col + ig_col
        m_new = jnp.maximum(f_total + m_prev, jnp.max(log_to_end, axis=0, keepdims=True))
        decay = jnp.exp(f_total + m_prev - m_new)
        w_end = jnp.exp(log_to_end - m_new)
        kw = kf * w_end
        c_ref[h] = decay * c_prev + _dot_tn(kw.astype(BF16), vh)
        n_ref[h:h + 1, :] = decay * n_prev + jnp.sum(kw, axis=0, keepdims=True)
        m_ref[h:h + 1, :] = jnp.broadcast_to(m_new, (1, LANES))

        hn = _rms(hout, og_ref[:, h * hd:(h + 1) * hd])
        o_pre = vo_ref[0, rows, ML_WIDTH + h * hd:ML_WIDTH + (h + 1) * hd].astype(F32)
        o_ref[0, rows, h * hd:(h + 1) * hd] = (jax.nn.sigmoid(o_pre) * hn).astype(o_ref.dtype)


def _conv_shift_matrix():
    L = ML_CHUNK
    sh = np.zeros(((ML_CONV - 1) * L, 2 * L), np.float32)
    for j in range(1, ML_CONV):
        sh[(j - 1) * L + np.arange(L), L + np.arange(L) - j] = 1.0
    return jnp.asarray(sh, BF16)


def _mlstm(mlqk, mlvo, gif, conv_w, conv_b, out_gain):
    b, s, _ = mlqk.shape
    tl = ML_TILE
    assert s % tl == 0
    shift = _conv_shift_matrix()
    blk = lambda w: pl.BlockSpec((1, tl, w), lambda bi, ci: (bi, ci, 0))
    return pl.pallas_call(
        _mlstm_body,
        grid=(b, s // tl),
        in_specs=[blk(W_MLQK), blk(W_MLVO), blk(W_IF), _const_spec(shift.shape),
                  _const_spec(conv_w.shape), _const_spec(conv_b.shape), _const_spec(out_gain.shape)],
        out_specs=blk(ML_WIDTH),
        out_shape=jax.ShapeDtypeStruct((b, s, ML_WIDTH), BF16),
        scratch_shapes=[pltpu.VMEM((ML_CHUNK + tl, W_MLQK), BF16),
                        pltpu.VMEM((ML_HEADS, ML_HEAD_DIM, ML_HEAD_DIM), F32),
                        pltpu.VMEM((8, ML_HEAD_DIM), F32),
                        pltpu.VMEM((8, LANES), F32)],
        compiler_params=_cparams(("parallel", "arbitrary")),
        name="mlstm",
    )(mlqk, mlvo, gif, shift, conv_w, conv_b, out_gain)


def _rope_table_body(pos_ref, invf_ref, sign_ref, cos_ref, sin_ref):
    ang = pos_ref[...].astype(F32) * invf_ref[...]
    cos_ref[...] = jnp.cos(ang)
    sin_ref[...] = jnp.sin(ang) * sign_ref[...]


_ROPE_HALF = MLA_ROPE // 2
_HEAD_SRC = np.concatenate([
    np.arange(MLA_NOPE, MLA_NOPE + _ROPE_HALF), np.arange(0, MLA_NOPE - _ROPE_HALF),
    np.arange(MLA_NOPE + _ROPE_HALF, MLA_QK), np.arange(MLA_NOPE - _ROPE_HALF, MLA_NOPE)])


def _head_lanes(a):
    a = a[..., _HEAD_SRC]
    return jnp.pad(a, [(0, 0)] * (a.ndim - 1) + [(0, LANES - MLA_QK)])


def _rope_tables(positions):
    t = positions.size
    half = _ROPE_HALF
    inv_freq = jnp.power(ROPE_THETA, -jnp.arange(half, dtype=F32) / half)
    zeros = jnp.zeros((half,), F32)
    nope = jnp.zeros((MLA_NOPE,), F32)
    invf = _head_lanes(jnp.concatenate([nope, inv_freq, inv_freq]))[None, :]
    sign = _head_lanes(jnp.concatenate([nope, zeros - 1.0, zeros + 1.0]))[None, :]
    tm = min(TM_DENSE, t)
    return pl.pallas_call(
        _rope_table_body,
        grid=(t // tm,),
        in_specs=[pl.BlockSpec((tm, 1), lambda i: (i, 0)), _const_spec(invf.shape), _const_spec(sign.shape)],
        out_specs=[pl.BlockSpec((tm, LANES), lambda i: (i, 0))] * 2,
        out_shape=[jax.ShapeDtypeStruct((t, LANES), F32)] * 2,
        compiler_params=_cparams(("parallel",)),
        name="rope_tables",
    )(positions.reshape(t, 1), invf, sign)


def _mla_prep_body(c_ref, cos_ref, sin_ref, qn_ref, kvn_ref, wq_ref, wk_ref, wv_ref, qg_ref, kg_ref,
                   q_out, k_out, v_out):
    c = c_ref[...].astype(F32)
    cq = _rms(c[:, :MLA_Q_RANK], qn_ref[...]).astype(BF16)
    ckv = _rms(c[:, MLA_Q_RANK:MLA_Q_RANK + MLA_KV_RANK], kvn_ref[...]).astype(BF16)
    k_rope = c[:, MLA_Q_RANK + MLA_KV_RANK:]
    q = _dot(cq, wq_ref[...])
    k = _dot(ckv, wk_ref[...])
    vlane = lax.broadcasted_iota(jnp.int32, (1, MLA_HEADS * LANES), 1)
    ones_half = jnp.where(vlane % LANES >= MLA_V, 1.0, 0.0)
    v_out[...] = (_dot(ckv, wv_ref[...]) + ones_half).astype(v_out.dtype)
    cos = cos_ref[...]
    sin = sin_ref[...]

    def norm_rope(x, gain):
        y = x * lax.rsqrt(jnp.sum(x * x, axis=-1, keepdims=True) * (1.0 / MLA_QK) + EPS) * gain
        return y * cos + pltpu.roll(y, LANES // 2, 1) * sin

    for h in range(MLA_HEADS):
        sl = slice(h * LANES, (h + 1) * LANES)
        q_out[:, sl] = norm_rope(q[:, sl], qg_ref[...]).astype(q_out.dtype)
        k_out[:, sl] = norm_rope(k[:, sl] + k_rope, kg_ref[...]).astype(k_out.dtype)


def _mla_prep(mla_c, cos, sin, qn, kvn, wq, wk, wv, qg, kg):
    t = mla_c.shape[0]
    tm = min(TM_DENSE, t)
    row = lambda w: pl.BlockSpec((tm, w), lambda i: (i, 0))
    consts = (qn, kvn, wq, wk, wv, qg, kg)
    widths = (MLA_HEADS * LANES,) * 3
    return pl.pallas_call(
        _mla_prep_body,
        grid=(t // tm,),
        in_specs=[row(W_MLA), row(LANES), row(LANES)] + [_const_spec(c.shape) for c in consts],
        out_specs=[row(w) for w in widths],
        out_shape=[jax.ShapeDtypeStruct((t, w), BF16) for w in widths],
        compiler_params=_cparams(("parallel",)),
        name="mla_prep",
    )(mla_c, cos, sin, *consts)


def _mla_attn_body(q_ref, k_ref, v_ref, o_ref, sa_ref, sb_ref, pa_ref, pb_ref, ala_ref, alb_ref, m_ref, acc_ref):
    qi = pl.program_id(2)
    tq, tk, rc = MLA_TQ, MLA_TK, MLA_ROWS
    m_ref[...] = jnp.full(m_ref.shape, -jnp.inf, F32)
    acc_ref[...] = jnp.zeros(acc_ref.shape, F32)
    pb_ref[...] = jnp.zeros(pb_ref.shape, BF16)
    alb_ref[...] = jnp.ones(alb_ref.shape, F32)

    def scores(blk, s_ref, row0=0):
        start = pl.multiple_of(blk * tk, tk)
        for h in range(2):
            hl = slice(h * LANES, (h + 1) * LANES)
            s_ref[h, row0:, :] = _dot_nt(q_ref[0, row0:, hl], k_ref[0, pl.ds(start, tk), hl])

    def softmax(s_ref, p_ref, al_ref, diag_block=None, row0=0):
        for h in range(2):
            for r in range(row0, tq, rc):
                s = s_ref[h, r:r + rc, :]
                if diag_block is not None and diag_block * tk + tk - 1 > r:
                    qpos = lax.broadcasted_iota(jnp.int32, (rc, tk), 0) + r
                    kpos = lax.broadcasted_iota(jnp.int32, (rc, tk), 1) + diag_block * tk
                    s = jnp.where(kpos <= qpos, s, -jnp.inf)
                m_old = m_ref[h, r:r + rc, :]
                m_new = jnp.maximum(m_old, jnp.max(s, axis=-1, keepdims=True))
                al_ref[h, r:r + rc, :] = jnp.exp2(m_old - m_new)
                m_ref[h, r:r + rc, :] = m_new
                for c in range(0, tk, LANES):
                    p_ref[h, r:r + rc, c:c + LANES] = jnp.exp2(s[:, c:c + LANES] - m_new).astype(BF16)

    def weighted_values(blk, p_ref, al_ref, row0=0):
        start = pl.multiple_of(blk * tk, tk)
        for h in range(2):
            hl = slice(h * LANES, (h + 1) * LANES)
            acc_ref[h, row0:, :] = (al_ref[h, row0:, :] * acc_ref[h, row0:, :]
                                    + _dot(p_ref[h, row0:, :], v_ref[0, pl.ds(start, tk), hl]))

    scores(0, sa_ref)

    def block_pair(j, carry):
        weighted_values(jnp.maximum(2 * j - 1, 0), pb_ref, alb_ref)
        scores(2 * j + 1, sb_ref)
        softmax(sa_ref, pa_ref, ala_ref)
        weighted_values(2 * j, pa_ref, ala_ref)
        scores(2 * j + 2, sa_ref)
        softmax(sb_ref, pb_ref, alb_ref)
        return carry

    lax.fori_loop(0, qi, block_pair, 0)
    weighted_values(jnp.maximum(2 * qi - 1, 0), pb_ref, alb_ref)
    scores(2 * qi + 1, sb_ref, row0=tk)
    softmax(sa_ref, pa_ref, ala_ref, diag_block=0)
    weighted_values(2 * qi, pa_ref, ala_ref)
    softmax(sb_ref, pb_ref, alb_ref, diag_block=1, row0=tk)
    weighted_values(2 * qi + 1, pb_ref, alb_ref, row0=tk)

    lane = lax.broadcasted_iota(jnp.int32, (1, LANES), 1)
    a0, a1 = acc_ref[0], acc_ref[1]
    o0 = a0 * pltpu.roll(1.0 / a0, MLA_V, 1)
    o1 = pltpu.roll(a1, MLA_V, 1) * (1.0 / a1)
    o_ref[0] = jnp.where(lane < MLA_V, o0, o1).astype(o_ref.dtype)


def _mla_attention(q, k, v):
    b, s, _ = q.shape
    tq = MLA_TQ
    assert s % tq == 0
    npair = MLA_HEADS // 2
    return pl.pallas_call(
        _mla_attn_body,
        grid=(b, npair, s // tq),
        in_specs=[pl.BlockSpec((1, tq, 2 * LANES), lambda bi, hp, qi: (bi, qi, hp)),
                  pl.BlockSpec((1, s, 2 * LANES), lambda bi, hp, qi: (bi, 0, hp)),
                  pl.BlockSpec((1, s, 2 * LANES), lambda bi, hp, qi: (bi, 0, hp))],
        out_specs=pl.BlockSpec((1, tq, LANES), lambda bi, hp, qi: (bi, qi, hp)),
        out_shape=jax.ShapeDtypeStruct((b, s, MLA_WIDTH), BF16),
        scratch_shapes=[pltpu.VMEM((2, tq, MLA_TK), F32), pltpu.VMEM((2, tq, MLA_TK), F32),
                        pltpu.VMEM((2, tq, MLA_TK), BF16), pltpu.VMEM((2, tq, MLA_TK), BF16),
                        pltpu.VMEM((2, tq, LANES), F32), pltpu.VMEM((2, tq, LANES), F32),
                        pltpu.VMEM((2, tq, LANES), F32), pltpu.VMEM((2, tq, LANES), F32)],
        compiler_params=_cparams(("parallel", "parallel", "arbitrary")),
        name="mla_attention",
    )(q, k, v)


def _seg(z, name):
    off = 0
    for n, w in SEGMENTS:
        if n == name:
            return z[..., off:off + w]
        off += w
    raise KeyError(name)


def _pack_inproj(w, b):
    wb = jnp.concatenate([w, b[None, :]], axis=0)
    z = lambda n: jnp.zeros((wb.shape[0], n), wb.dtype)
    kr = _head_lanes(jnp.concatenate([z(MLA_NOPE), _seg(wb, 'mla_kr')], axis=1))
    gif = jnp.concatenate([_seg(wb, 'ml_i'), _seg(wb, 'ml_f'), z(LANES - 2 * ML_HEADS)], axis=1)
    packed = jnp.concatenate(
        [_seg(wb, 'gates'), _seg(wb, 'ml_q'), _seg(wb, 'ml_k'), _seg(wb, 'ml_v'), _seg(wb, 'ml_o'),
         _seg(wb, 'sb_q'), _seg(wb, 'sb_k'), _seg(wb, 'sb_v'), _seg(wb, 'mla_cq'), _seg(wb, 'mla_ckv'),
         kr, gif], axis=1)
    return packed[:-1].astype(BF16), packed[-1:]


def _mla_up_weights(wq_up, wkv_up):
    hq = wq_up.reshape(MLA_Q_RANK, MLA_HEADS, MLA_QK)
    hkv = wkv_up.reshape(MLA_KV_RANK, MLA_HEADS, MLA_NOPE + MLA_V)
    k_nope = jnp.pad(hkv[:, :, :MLA_NOPE], ((0, 0), (0, 0), (0, MLA_ROPE)))
    v = jnp.pad(hkv[:, :, MLA_NOPE:], ((0, 0), (0, 0), (0, LANES - MLA_V)))
    flat = lambda a: a.reshape(a.shape[0], MLA_HEADS * LANES).astype(BF16)
    return flat(_head_lanes(hq)), flat(_head_lanes(k_nope)), flat(v)


def kernel(x, positions, ffn1_norm, ffn1_wi, ffn1_wo, mix_norm, w_in, b_in, ml_conv_w, ml_conv_b,
           ml_out_norm, mla_q_norm, mla_kv_norm, mla_wq_up, mla_wkv_up, mla_q_gain, mla_k_gain,
           w_up_sb, w_up_ml, w_up_mla, w_out, ffn2_norm, ffn2_wi, ffn2_wo):
    b, s, d = x.shape
    t = b * s
    depth = w_in.shape[0]
    cos, sin = _rope_tables(positions)
    xt = x.reshape(t, d)
    row = lambda a: a[None, :]
    for l in range(depth):
        w_packed, b_packed = _pack_inproj(w_in[l], b_in[l])
        xt, gates, mlqk, mlvo, sbqkv, mla_c, gif = _ffn_inproj(
            xt, row(ffn1_norm[l]), ffn1_wi[l].astype(BF16), ffn1_wo[l].astype(BF16),
            row(mix_norm[l]), w_packed, b_packed)

        ysb = _sb_attention(sbqkv.reshape(b, s, W_SB))

        yml = _mlstm(mlqk.reshape(b, s, W_MLQK), mlvo.reshape(b, s, W_MLVO), gif.reshape(b, s, W_IF),
                     ml_conv_w[l], row(ml_conv_b[l]), row(ml_out_norm[l]))

        q_scale = MLA_QK ** -0.5 * np.log2(np.e)
        qh, kh, vh = _mla_prep(
            mla_c, cos, sin, row(mla_q_norm[l]), row(mla_kv_norm[l]),
            *_mla_up_weights(mla_wq_up[l], mla_wkv_up[l]),
            row(_head_lanes(mla_q_gain[l] * q_scale)), row(_head_lanes(mla_k_gain[l])))
        ymla = _mla_attention(qh.reshape(b, s, -1), kh.reshape(b, s, -1), vh.reshape(b, s, -1))

        xt = _merge_ffn(xt, gates, ysb.reshape(t, -1), yml.reshape(t, -1), ymla.reshape(t, -1),
                        w_up_sb[l].astype(BF16), w_up_ml[l].astype(BF16), w_up_mla[l].astype(BF16),
                        w_out[l].astype(BF16), row(ffn2_norm[l]), ffn2_wi[l].astype(BF16), ffn2_wo[l].astype(BF16))
    return xt.reshape(b, s, d)
```
